```python
import math
import jax, jax.numpy as jnp
from jax import lax
import numpy as np

D_MODEL = 2048
BATCH = 4
SEQ = 8192
DEPTH = 1

CHUNK = 64
Q_BLOCK = 128
ROPE_THETA = 10000.0
EPS = 1e-6

DA_HEADS = 8
DA_HEAD_DIM = 128
DA_V_DIM = 2 * DA_HEAD_DIM
DA_QK_WIDTH = DA_HEADS * 2 * DA_HEAD_DIM
DA_WIDTH = DA_HEADS * DA_V_DIM

RW_HEAD_DIM = 64
RW_HEADS = D_MODEL // RW_HEAD_DIM
RW_WIDTH = RW_HEADS * RW_HEAD_DIM
RW_DECAY_RANK = max(32, int(round(math.sqrt(D_MODEL) * 1.8 / 32)) * 32)
RW_AAA_RANK = max(32, int(round(math.sqrt(D_MODEL) * 1.8 / 32)) * 32)
RW_GATE_RANK = max(32, int(round(0.6 * D_MODEL ** 0.8 / 32)) * 32)
RW_GN_EPS = 1e-5 * RW_HEAD_DIM
RW_TOTAL = 3 * RW_WIDTH + RW_DECAY_RANK + RW_AAA_RANK + RW_GATE_RANK

D_FF = -(-8 * D_MODEL // (3 * 256)) * 256

OFF_DA_Q = 0
OFF_DA_K = OFF_DA_Q + DA_QK_WIDTH
OFF_DA_V = OFF_DA_K + DA_QK_WIDTH
OFF_RW = OFF_DA_V + DA_WIDTH
OFF_GATE = OFF_RW + RW_TOTAL
N_IN = OFF_GATE + 2 * D_MODEL

kernel_name = "hybrid_diffattn_rwkv7_gated_block"


def _rmsnorm(x, g, eps=EPS):
    xf = x.astype(jnp.float32)
    y = xf * lax.rsqrt(jnp.mean(xf * xf, axis=-1, keepdims=True) + eps)
    return (y * g.astype(jnp.float32)).astype(x.dtype)


def _rope(t, pos):
    half = t.shape[-1] // 2
    inv = ROPE_THETA ** (-jnp.arange(half, dtype=jnp.float32) / half)
    ang = pos.astype(jnp.float32)[:, None] * inv[None, :]
    cos = jnp.cos(ang)[:, None, None, :]
    sin = jnp.sin(ang)[:, None, None, :]
    tf = t.astype(jnp.float32)
    t1, t2 = tf[..., :half], tf[..., half:]
    out = jnp.concatenate([t1 * cos - t2 * sin, t2 * cos + t1 * sin], axis=-1)
    return out.astype(t.dtype)


def _diff_attention(q, k, v, lam, subln_g, lambda_init):
    B, S = q.shape[0], q.shape[1]
    n_blocks = S // Q_BLOCK
    scale = DA_HEAD_DIM ** -0.5
    key_chunk = jnp.arange(S) // CHUNK
    qb = jnp.moveaxis(q.reshape(B, n_blocks, Q_BLOCK, DA_HEADS, 2, DA_HEAD_DIM), 1, 0)
    neg = jnp.finfo(jnp.float32).min

    def one_block(args):
        q_blk, i = args
        q_chunk = (i * Q_BLOCK + jnp.arange(Q_BLOCK)) // CHUNK
        mask = key_chunk[None, :] <= q_chunk[:, None]
        s = jnp.einsum('bqhcd,bkhcd->bhcqk', q_blk, k).astype(jnp.float32) * scale
        s = jnp.where(mask, s, neg)
        p = jax.nn.softmax(s, axis=-1)
        p = p[:, :, 0] - lam * p[:, :, 1]
        return jnp.einsum('bhqk,bkhe->bqhe', p.astype(v.dtype), v)

    o = lax.map(one_block, (qb, jnp.arange(n_blocks)))
    o = jnp.moveaxis(o, 0, 1).reshape(B, S, DA_HEADS, DA_V_DIM)
    o = _rmsnorm(o, subln_g, eps=1e-5) * (1.0 - lambda_init)
    return o.reshape(B, S, DA_WIDTH)


def _rwkv7_step(state, inp):
    r_t, w_t, k_t, v_t, a_t, b_t = inp
    sa = jnp.einsum('bhij,bhj->bhi', state, a_t)
    state = (state * w_t[:, :, None, :] + sa[..., None] * b_t[:, :, None, :]
             + v_t[..., None] * k_t[:, :, None, :])
    y = jnp.einsum('bhij,bhj->bhi', state, r_t)
    return state, y


def _rwkv7(p, mu, w0, w_up, a0, a_up, g_up, k_k, k_a, r_k, ln_w, ln_b):
    B, S, _ = p.shape
    prev = jnp.pad(p, ((0, 0), (1, 0), (0, 0)))[:, :S]
    xs = p + mu * (prev - p)
    o0, o1, o2, o3 = RW_WIDTH, 2 * RW_WIDTH, 3 * RW_WIDTH, 3 * RW_WIDTH + RW_DECAY_RANK
    o4 = o3 + RW_AAA_RANK
    r, k, v = xs[..., :o0], xs[..., o0:o1], xs[..., o1:o2]
    wl, al, gl = xs[..., o2:o3], xs[..., o3:o4], xs[..., o4:]
    w = -jax.nn.softplus(-(w0 + jnp.tanh(wl) @ w_up)) - 0.5
    decay = jnp.exp(-jnp.exp(w.astype(jnp.float32)))
    a = jax.nn.sigmoid(a0 + al @ a_up)
    g = jax.nn.sigmoid(gl) @ g_up

    def heads(t):
        return t.reshape(B, S, RW_HEADS, RW_HEAD_DIM).astype(jnp.float32)

    kk = heads(k * k_k)
    kk = kk / jnp.maximum(jnp.linalg.norm(kk, axis=-1, keepdims=True), 1e-12)
    k = k * (1.0 + (a - 1.0) * k_a)
    rh, kh, vh, ah, wh = heads(r), heads(k), heads(v), heads(a), heads(decay)

    def tm(t):
        return jnp.swapaxes(t, 0, 1)

    state0 = jnp.zeros((B, RW_HEADS, RW_HEAD_DIM, RW_HEAD_DIM), jnp.float32)
    _, y = lax.scan(_rwkv7_step, state0,
                    (tm(rh), tm(wh), tm(kh), tm(vh), tm(-kk), tm(kk * ah)))
    y = tm(y)
    mean = jnp.mean(y, axis=-1, keepdims=True)
    var = jnp.mean(jnp.square(y - mean), axis=-1, keepdims=True)
    y = ((y - mean) * lax.rsqrt(var + RW_GN_EPS)).reshape(B, S, RW_WIDTH)
    y = y * ln_w.astype(jnp.float32) + ln_b.astype(jnp.float32)
    bonus = jnp.sum(rh * kh * r_k.astype(jnp.float32), axis=-1, keepdims=True) * vh
    y = y + bonus.reshape(B, S, RW_WIDTH)
    return (y * g.astype(jnp.float32)).astype(p.dtype)


def setup_inputs(seed: int = 0) -> dict:
    key = jax.random.key(seed)
    ks = jax.random.split(key, 28)
    L = DEPTH

    def nrm(k, shape, scale):
        return jax.random.normal(k, shape, jnp.float32) * scale

    return {
        "x": nrm(ks[0], (BATCH, SEQ, D_MODEL), 1.0),
        "w_in": nrm(ks[1], (L, D_MODEL, N_IN), D_MODEL ** -0.5),
        "b_gate": nrm(ks[2], (L, 2 * D_MODEL), 0.01),
        "g_mix_pre": 1.0 + nrm(ks[3], (L, D_MODEL), 0.05),
        "g_mix_post": 1.0 + nrm(ks[4], (L, D_MODEL), 0.05),
        "g_ffn_pre": 1.0 + nrm(ks[5], (L, D_MODEL), 0.05),
        "g_ffn_post": 1.0 + nrm(ks[6], (L, D_MODEL), 0.05),
        "da_lambda_q": nrm(ks[7], (L, 2, DA_HEAD_DIM), 0.1),
        "da_lambda_k": nrm(ks[8], (L, 2, DA_HEAD_DIM), 0.1),
        "da_subln_g": 1.0 + nrm(ks[9], (L, DA_V_DIM), 0.05),
        "rw_mu": jax.random.uniform(ks[10], (L, RW_TOTAL), jnp.float32),
        "rw_w0": jax.random.uniform(ks[11], (L, RW_WIDTH), jnp.float32, minval=-6.0, maxval=0.0),
        "rw_w_up": nrm(ks[12], (L, RW_DECAY_RANK, RW_WIDTH), 0.5 * RW_DECAY_RANK ** -0.5),
        "rw_a0": nrm(ks[13], (L, RW_WIDTH), 0.1),
        "rw_a_up": nrm(ks[14], (L, RW_AAA_RANK, RW_WIDTH), 0.5 * RW_AAA_RANK ** -0.5),
        "rw_g_up": nrm(ks[15], (L, RW_GATE_RANK, RW_WIDTH), RW_GATE_RANK ** -0.5),
        "rw_k_k": 0.85 + nrm(ks[16], (L, RW_WIDTH), 0.05),
        "rw_k_a": 1.0 + nrm(ks[17], (L, RW_WIDTH), 0.05),
        "rw_r_k": nrm(ks[18], (L, RW_HEADS, RW_HEAD_DIM), 0.1),
        "rw_ln_w": 1.0 + nrm(ks[19], (L, RW_WIDTH), 0.05),
        "rw_ln_b": nrm(ks[20], (L, RW_WIDTH), 0.01),
        "w_branch_a": nrm(ks[21], (L, DA_WIDTH, D_MODEL), DA_WIDTH ** -0.5),
        "w_branch_b": nrm(ks[22], (L, RW_WIDTH, D_MODEL), RW_WIDTH ** -0.5),
        "w_out": nrm(ks[23], (L, D_MODEL, D_MODEL), D_MODEL ** -0.5),
        "w_ffn_in": nrm(ks[24], (L, D_MODEL, 2 * D_FF), D_MODEL ** -0.5),
        "w_ffn_out": nrm(ks[25], (L, D_FF, D_MODEL), D_FF ** -0.5),
    }


def reference(x, w_in, b_gate, g_mix_pre, g_mix_post, g_ffn_pre, g_ffn_post,
              da_lambda_q, da_lambda_k, da_subln_g,
              rw_mu, rw_w0, rw_w_up, rw_a0, rw_a_up, rw_g_up, rw_k_k, rw_k_a, rw_r_k,
              rw_ln_w, rw_ln_b, w_branch_a, w_branch_b, w_out, w_ffn_in, w_ffn_out):
    B, S, _ = x.shape
    pos = jnp.arange(S, dtype=jnp.int32)
    for l in range(DEPTH):
        lambda_init = 0.8 - 0.6 * math.exp(-0.3 * l)
        h = _rmsnorm(x, g_mix_pre[l])
        proj = h @ w_in[l]
        q = _rope(proj[..., OFF_DA_Q:OFF_DA_K].reshape(B, S, DA_HEADS, 2, DA_HEAD_DIM), pos)
        k = _rope(proj[..., OFF_DA_K:OFF_DA_V].reshape(B, S, DA_HEADS, 2, DA_HEAD_DIM), pos)
        v = proj[..., OFF_DA_V:OFF_RW].reshape(B, S, DA_HEADS, DA_V_DIM)
        lq = da_lambda_q[l].astype(jnp.float32)
        lk = da_lambda_k[l].astype(jnp.float32)
        lam = jnp.exp(jnp.sum(lq[0] * lk[0])) - jnp.exp(jnp.sum(lq[1] * lk[1])) + lambda_init
        o_a = _diff_attention(q, k, v, lam, da_subln_g[l], lambda_init)
        o_b = _rwkv7(proj[..., OFF_RW:OFF_GATE], rw_mu[l], rw_w0[l], rw_w_up[l], rw_a0[l],
                     rw_a_up[l], rw_g_up[l], rw_k_k[l], rw_k_a[l], rw_r_k[l],
                     rw_ln_w[l], rw_ln_b[l])
        gates = jax.nn.sigmoid(proj[..., OFF_GATE:] + b_gate[l])
        mix = (gates[..., :D_MODEL] * (o_a @ w_branch_a[l])
               + gates[..., D_MODEL:] * (o_b @ w_branch_b[l])) @ w_out[l]
        x = x + _rmsnorm(mix, g_mix_post[l])
        h = _rmsnorm(x, g_ffn_pre[l])
        gu = h @ w_ffn_in[l]
        f = jax.nn.silu(gu[..., :D_FF]) * gu[..., D_FF:]
        x = x + _rmsnorm(f @ w_ffn_out[l], g_ffn_post[l])
    return x
```

```python
import functools
import math

import jax
import jax.numpy as jnp
from jax import lax
from jax.experimental import pallas as pl
from jax.experimental.pallas import tpu as pltpu

F32 = jnp.float32
BF16 = jnp.bfloat16

D_MODEL = 2048
EPS = 1e-6
ROPE_THETA = 10000.0
ATTN_CHUNK = 64
DA_HEADS = 8
DA_HEAD_DIM = 128
DA_V_DIM = 2 * DA_HEAD_DIM
SUBLN_EPS = 1e-5
LAMBDA_INIT = 0.8 - 0.6 * math.exp(-0.3 * 0)
RW_HEAD_DIM = 64
RW_DECAY_RANK = 96
RW_AAA_RANK = 96
RW_GATE_RANK = 256
RW_GN_EPS = 1e-5 * RW_HEAD_DIM
D_FF = 5632

LANES = 128
MXU_DIM = 256
VMEM_LIMIT = 56 * 1024 * 1024

LORA_PAD = LANES
RW_R, RW_K, RW_V = 0, D_MODEL, 2 * D_MODEL
RW_WL = 3 * D_MODEL
RW_AL = RW_WL + LORA_PAD
RW_GL = RW_AL + LORA_PAD
RW_P = RW_GL + RW_GATE_RANK
P_RW = 0
P_Q = P_RW + RW_P
P_K = P_Q + D_MODEL
P_V = P_K + D_MODEL
P_GATE = P_V + D_MODEL
N_PROJ = P_GATE + 2 * D_MODEL

RW_CHUNK = 64
HEADS_PER_GROUP = MXU_DIM // RW_HEAD_DIM
N_GROUPS = D_MODEL // MXU_DIM
NEG_BIG = -1e30


def _params(n_axes):
    return pltpu.CompilerParams(dimension_semantics=("arbitrary",) * n_axes,
                                vmem_limit_bytes=VMEM_LIMIT)


def _dot(a, b):
    return jnp.dot(a, b, preferred_element_type=F32)


def _dot_nt(a, b):
    return lax.dot_general(a, b, (((1,), (1,)), ((), ())), preferred_element_type=F32)


def _dot_tn(a, b):
    return lax.dot_general(a, b, (((0,), (0,)), ((), ())), preferred_element_type=F32)


def _split_bf16(x):
    hi = x.astype(BF16)
    lo = (x - hi.astype(F32)).astype(BF16)
    return hi, lo


def _rms(x, eps):
    return x * lax.rsqrt(jnp.mean(x * x, axis=-1, keepdims=True) + eps)


def _sigmoid(x):
    return 1.0 / (1.0 + jnp.exp(-x))


def _iota(shape, axis):
    return lax.broadcasted_iota(jnp.int32, shape, axis)


def _proj_kernel(x_ref, g_ref, w_ref, cos_ref, sin_ref, o_ref, hn_ref, *, tn, scale):
    j = pl.program_id(1)
    q_lo, k_lo, v_lo = P_Q // tn, P_K // tn, P_V // tn

    @pl.when(j == 0)
    def _():
        hn_ref[...] = (_rms(x_ref[...], EPS) * g_ref[...]).astype(BF16)

    acc = _dot(hn_ref[...], w_ref[...])
    is_rope = (j >= q_lo) & (j < v_lo)

    @pl.when(is_rope)
    def _():
        cos = cos_ref[...]
        sin = sin_ref[...]
        sc = jnp.where(j < k_lo, scale, 1.0).astype(F32)
        for c in range(tn // LANES):
            t = acc[:, c * LANES:(c + 1) * LANES]
            o = (t * cos + pltpu.roll(t, LANES // 2, 1) * sin) * sc
            o_ref[:, c * LANES:(c + 1) * LANES] = o.astype(o_ref.dtype)

    @pl.when(jnp.logical_not(is_rope))
    def _():
        o_ref[...] = acc.astype(o_ref.dtype)


def _proj(x2, g, w_p, cos_t, sin_t, seq):
    m = x2.shape[0]
    tm = min(1024, seq)
    tn = 512
    n_seq_tiles = seq // tm
    return pl.pallas_call(
        functools.partial(_proj_kernel, tn=tn, scale=DA_HEAD_DIM ** -0.5),
        out_shape=jax.ShapeDtypeStruct((m, N_PROJ), BF16),
        grid=(m // tm, N_PROJ // tn),
        in_specs=[
            pl.BlockSpec((tm, D_MODEL), lambda i, j: (i, 0)),
            pl.BlockSpec((1, D_MODEL), lambda i, j: (0, 0)),
            pl.BlockSpec((D_MODEL, tn), lambda i, j: (0, j)),
            pl.BlockSpec((tm, LANES), lambda i, j: (i % n_seq_tiles, 0)),
            pl.BlockSpec((tm, LANES), lambda i, j: (i % n_seq_tiles, 0)),
        ],
        out_specs=pl.BlockSpec((tm, tn), lambda i, j: (i, j)),
        scratch_shapes=[pltpu.VMEM((tm, D_MODEL), BF16)],
        compiler_params=_params(2),
        name="proj",
    )(x2, g, w_p, cos_t, sin_t)


def _attn_kernel(lq_ref, lk_ref, g_ref, q_ref, k_ref, v_ref, o_ref,
                 m1, l1, a1, m2, l2, a2, *, tq):
    qi = pl.program_id(2)
    stats = ((m1, l1, a1), (m2, l2, a2))
    for m_ref, l_ref, a_ref in stats:
        m_ref[...] = jnp.full(m_ref.shape, NEG_BIG, F32)
        l_ref[...] = jnp.zeros(l_ref.shape, F32)
        a_ref[...] = jnp.zeros(a_ref.shape, F32)

    def block(kb, mask):
        start = pl.multiple_of(kb * tq, tq)
        kblk = k_ref[pl.ds(start, tq), :]
        vblk = v_ref[pl.ds(start, tq), :]
        for c, (m_ref, l_ref, a_ref) in enumerate(stats):
            q = q_ref[:, c * DA_HEAD_DIM:(c + 1) * DA_HEAD_DIM]
            k = kblk[:, c * DA_HEAD_DIM:(c + 1) * DA_HEAD_DIM]
            s = _dot_nt(q, k)
            if mask is not None:
                s = jnp.where(mask, s, NEG_BIG)
            m_prev = m_ref[...]
            m_new = jnp.maximum(m_prev, jnp.max(s, axis=1, keepdims=True))
            alpha = jnp.exp(m_prev - m_new)
            p = jnp.exp(s - m_new)
            l_ref[...] = alpha * l_ref[...] + jnp.sum(p, axis=1, keepdims=True)
            a_ref[...] = alpha * a_ref[...] + _dot(p.astype(BF16), vblk)
            m_ref[...] = m_new

    def full_block(kb, carry):
        block(kb, None)
        return carry

    lax.fori_loop(0, qi, full_block, 0)
    row_chunk = _iota((tq, tq), 0) // ATTN_CHUNK
    col_chunk = _iota((tq, tq), 1) // ATTN_CHUNK
    block(qi, col_chunk <= row_chunk)

    e = jnp.exp(jnp.sum(lq_ref[...] * lk_ref[...], axis=1, keepdims=True))
    lam = e[0:1, :] - e[1:2, :] + LAMBDA_INIT
    o = a1[...] / l1[...] - lam * (a2[...] / l2[...])
    o = _rms(o, SUBLN_EPS) * g_ref[...] * (1.0 - LAMBDA_INIT)
    o_ref[...] = o.astype(o_ref.dtype)


def _attn(proj, lq, lk, subln_g, batch, seq):
    m = proj.shape[0]
    tq = min(512, seq)
    nq = seq // tq
    qcol, kcol, vcol = P_Q // DA_V_DIM, P_K // DA_V_DIM, P_V // DA_V_DIM
    return pl.pallas_call(
        functools.partial(_attn_kernel, tq=tq),
        out_shape=jax.ShapeDtypeStruct((m, D_MODEL), BF16),
        grid=(batch, DA_HEADS, nq),
        in_specs=[
            pl.BlockSpec((2, DA_HEAD_DIM), lambda b, h, i: (0, 0)),
            pl.BlockSpec((2, DA_HEAD_DIM), lambda b, h, i: (0, 0)),
            pl.BlockSpec((1, DA_V_DIM), lambda b, h, i: (0, 0)),
            pl.BlockSpec((tq, DA_V_DIM), lambda b, h, i: (b * nq + i, qcol + h)),
            pl.BlockSpec((seq, DA_V_DIM), lambda b, h, i: (b, kcol + h)),
            pl.BlockSpec((seq, DA_V_DIM), lambda b, h, i: (b, vcol + h)),
        ],
        out_specs=pl.BlockSpec((tq, DA_V_DIM), lambda b, h, i: (b * nq + i, h)),
        scratch_shapes=[
            pltpu.VMEM((tq, 1), F32), pltpu.VMEM((tq, 1), F32), pltpu.VMEM((tq, DA_V_DIM), F32),
            pltpu.VMEM((tq, 1), F32), pltpu.VMEM((tq, 1), F32), pltpu.VMEM((tq, DA_V_DIM), F32),
        ],
        compiler_params=_params(3),
        name="attn",
    )(lq, lk, subln_g, proj, proj, proj)


def _head_ones():
    r = _iota((MXU_DIM, MXU_DIM), 0) // RW_HEAD_DIM
    c = _iota((MXU_DIM, MXU_DIM), 1) // RW_HEAD_DIM
    return r == c


def _group_sum(x, ones_bd):
    outs = []
    for gi in range(x.shape[1] // MXU_DIM):
        hi, lo = _split_bf16(x[:, gi * MXU_DIM:(gi + 1) * MXU_DIM])
        outs.append(_dot(hi, ones_bd) + _dot(lo, ones_bd))
    return jnp.concatenate(outs, axis=1)


def _prep_kernel(p_ref, mu_ref, w0_ref, wup_ref, a0_ref, aup_ref, gup_ref, kk_ref, ka_ref, rk_ref,
                 rt_ref, at_ref, bt_ref, kt_ref, bh_ref, kh_ref, v_ref, g_ref, bonus_ref, gl_ref,
                 carry_ref, *, tt, n_seq_tiles):
    i = pl.program_id(0)

    @pl.when(i % n_seq_tiles == 0)
    def _():
        carry_ref[...] = jnp.zeros(carry_ref.shape, F32)

    p = p_ref[...].astype(F32)
    row = _iota(p.shape, 0)
    prev = jnp.where(row == 0, carry_ref[...], pltpu.roll(p, 1, 0))
    carry_ref[...] = p[tt - 1:tt, :]
    xs = p + mu_ref[...] * (prev - p)

    r = xs[:, RW_R:RW_R + D_MODEL]
    k = xs[:, RW_K:RW_K + D_MODEL]
    v = xs[:, RW_V:RW_V + D_MODEL]
    wl = xs[:, RW_WL:RW_WL + LORA_PAD]
    al = xs[:, RW_AL:RW_AL + LORA_PAD]
    gl = xs[:, RW_GL:RW_GL + RW_GATE_RANK]

    z = w0_ref[...] + _dot(jnp.tanh(wl).astype(BF16), wup_ref[...])
    logd = -math.exp(-0.5) * _sigmoid(z)
    alr = _sigmoid(a0_ref[...] + _dot(al.astype(BF16), aup_ref[...]))
    g = _dot(_sigmoid(gl).astype(BF16), gup_ref[...])

    ones_bd = jnp.where(_head_ones(), 1.0, 0.0).astype(BF16)
    kk = k * kk_ref[...]
    kk = kk / jnp.maximum(jnp.sqrt(_group_sum(kk * kk, ones_bd)), 1e-12)
    kmod = k * (1.0 + (alr - 1.0) * ka_ref[...])
    bonus = _group_sum(r * kmod * rk_ref[...], ones_bd) * v
    b = kk * alr

    tr = _iota((tt, tt), 0)
    tc = _iota((tt, tt), 1)
    same = (tr // RW_CHUNK) == (tc // RW_CHUNK)
    tril = jnp.where(same & (tc <= tr), 1.0, 0.0).astype(BF16)
    blk = jnp.where(same, 1.0, 0.0).astype(BF16)
    hi, lo = _split_bf16(logd)
    c = _dot(tril, hi) + _dot(tril, lo)
    c_tot = _dot(blk, hi) + _dot(blk, lo)

    e_neg = jnp.exp(-c)
    e_end = jnp.exp(c_tot - c)
    rt_ref[...] = (r * jnp.exp(c)).astype(BF16)
    at_ref[...] = (-kk * jnp.exp(c - logd)).astype(BF16)
    bt_ref[...] = (b * e_neg).astype(BF16)
    kt_ref[...] = (kmod * e_neg).astype(BF16)
    bh_ref[...] = (b * e_end).astype(BF16)
    kh_ref[...] = (kmod * e_end).astype(BF16)
    v_ref[...] = v.astype(BF16)
    g_ref[...] = g.astype(BF16)
    bonus_ref[...] = bonus.astype(BF16)
    for ci in range(tt // RW_CHUNK):
        last = ci * RW_CHUNK + RW_CHUNK - 1
        gl_ref[ci] = jnp.exp(c[last:last + 1, :])


def _prep(proj, mu, w0, wup, a0, aup, gup, k_k, k_a, r_k, seq):
    m = proj.shape[0]
    tt = min(128, seq)
    tok = lambda i: (i, 0)
    vec = lambda i: (0, 0)
    big = jax.ShapeDtypeStruct((m, D_MODEL), BF16)
    return pl.pallas_call(
        functools.partial(_prep_kernel, tt=tt, n_seq_tiles=seq // tt),
        out_shape=[big] * 9 + [jax.ShapeDtypeStruct((m // RW_CHUNK, 1, D_MODEL), F32)],
        grid=(m // tt,),
        in_specs=[
            pl.BlockSpec((tt, RW_P), lambda i: (i, P_RW // RW_P)),
            pl.BlockSpec((1, RW_P), vec),
            pl.BlockSpec((1, D_MODEL), vec),
            pl.BlockSpec((LORA_PAD, D_MODEL), vec),
            pl.BlockSpec((1, D_MODEL), vec),
            pl.BlockSpec((LORA_PAD, D_MODEL), vec),
            pl.BlockSpec((RW_GATE_RANK, D_MODEL), vec),
            pl.BlockSpec((1, D_MODEL), vec),
            pl.BlockSpec((1, D_MODEL), vec),
            pl.BlockSpec((1, D_MODEL), vec),
        ],
        out_specs=[pl.BlockSpec((tt, D_MODEL), tok)] * 9
        + [pl.BlockSpec((tt // RW_CHUNK, 1, D_MODEL), lambda i: (i, 0, 0))],
        scratch_shapes=[pltpu.VMEM((1, RW_P), F32)],
        compiler_params=_params(1),
        name="rwkv_prep",
    )(proj, mu, w0, wup, a0, aup, gup, k_k, k_a, r_k)


def _stack_heads(x, lane_head):
    zero = jnp.zeros_like(x)
    return jnp.concatenate([jnp.where(lane_head == h, x, zero) for h in range(HEADS_PER_GROUP)], axis=0)


def _chunk_kernel(rt_ref, at_ref, bt_ref, kt_ref, bh_ref, kh_ref, v_ref, gl_ref, y_ref, h_ref):
    L = RW_CHUNK

    @pl.when(pl.program_id(1) == 0)
    def _():
        h_ref[...] = jnp.zeros(h_ref.shape, F32)

    row = _iota((L, MXU_DIM), 0)
    col = _iota((L, MXU_DIM), 1)
    lane_head = col // RW_HEAD_DIM
    src = col % L
    strict = src < row
    incl = src <= row
    eye_packed = jnp.where(src == row, 1.0, 0.0).astype(F32)
    r256 = _iota((MXU_DIM, MXU_DIM), 0)
    c256 = _iota((MXU_DIM, MXU_DIM), 1)
    eye256 = r256 == c256
    same_head = (r256 // RW_HEAD_DIM) == (c256 // RW_HEAD_DIM)
    gamma = gl_ref[0]

    def stack(x):
        return _stack_heads(x.astype(BF16), lane_head)

    for gi in range(N_GROUPS):
        sl = slice(gi * MXU_DIM, (gi + 1) * MXU_DIM)
        rt, at, bt, kt = rt_ref[:, sl], at_ref[:, sl], bt_ref[:, sl], kt_ref[:, sl]
        bh, kh, v = bh_ref[:, sl], kh_ref[:, sl], v_ref[:, sl]

        pair = _dot_nt(jnp.concatenate([at, rt], axis=0),
                       jnp.concatenate([stack(bt), stack(kt)], axis=0))
        a_ab = jnp.where(strict, pair[:L, :MXU_DIM], 0.0)
        a_ak = jnp.where(strict, pair[:L, MXU_DIM:], 0.0)
        a_rb = jnp.where(incl, pair[L:, :MXU_DIM], 0.0)
        a_rk = jnp.where(incl, pair[L:, MXU_DIM:], 0.0)

        pw = _dot(a_ab.astype(BF16), stack(a_ab))
        t_inv = eye_packed + a_ab
        n_log = int(math.log2(L))
        for rd in range(1, n_log):
            pwb = pw.astype(BF16)
            if rd < n_log - 1:
                prod = _dot(pwb, jnp.concatenate([stack(pw), stack(t_inv)], axis=1))
                pw = prod[:, :MXU_DIM]
                t_inv = t_inv + prod[:, MXU_DIM:]
            else:
                t_inv = t_inv + _dot(pwb, stack(t_inv))
        sv = stack(v)
        v_loc = _dot(a_ak.astype(BF16), sv)
        sol = _dot(t_inv.astype(BF16), jnp.concatenate([stack(at), stack(v_loc)], axis=1))
        a_eff, u_loc = sol[:, :MXU_DIM], sol[:, MXU_DIM:]
        mix = _dot(a_rb.astype(BF16), jnp.concatenate([stack(a_eff), stack(u_loc)], axis=1))
        r_eff = rt.astype(F32) + mix[:, :MXU_DIM]
        y_loc = mix[:, MXU_DIM:] + _dot(a_rk.astype(BF16), sv)

        h0 = h_ref[gi]
        from_state = _dot(jnp.concatenate([a_eff, r_eff], axis=0).astype(BF16), h0.astype(BF16))
        u = from_state[:L] + u_loc
        y_ref[:, sl] = from_state[L:] + y_loc

        upd = _dot_tn(jnp.concatenate([bh, kh], axis=0),
                      jnp.concatenate([u.astype(BF16), v], axis=0))
        g_row = jnp.broadcast_to(gamma[:, sl], (MXU_DIM, MXU_DIM))
        g_col = jnp.sum(jnp.where(eye256, g_row, 0.0), axis=1, keepdims=True)
        h_ref[gi] = g_col * h0 + jnp.where(same_head, upd, 0.0)


def _chunk(rt, at, bt, kt, bh, kh, v, gl, batch, seq):
    m = rt.shape[0]
    nc = seq // RW_CHUNK
    tok = pl.BlockSpec((RW_CHUNK, D_MODEL), lambda b, c: (b * nc + c, 0))
    return pl.pallas_call(
        _chunk_kernel,
        out_shape=jax.ShapeDtypeStruct((m, D_MODEL), F32),
        grid=(batch, nc),
        in_specs=[tok] * 7 + [pl.BlockSpec((1, 1, D_MODEL), lambda b, c: (b * nc + c, 0, 0))],
        out_specs=tok,
        scratch_shapes=[pltpu.VMEM((N_GROUPS, MXU_DIM, MXU_DIM), F32)],
        compiler_params=_params(2),
        name="rwkv_chunk",
    )(rt, at, bt, kt, bh, kh, v, gl)


def _post_kernel(y_ref, bonus_ref, g_ref, lnw_ref, lnb_ref, o_ref):
    ones_bd = jnp.where(_head_ones(), 1.0, 0.0).astype(BF16)
    y = y_ref[...]
    mean = _group_sum(y, ones_bd) * (1.0 / RW_HEAD_DIM)
    d = y - mean
    var = _group_sum(d * d, ones_bd) * (1.0 / RW_HEAD_DIM)
    yn = d * lax.rsqrt(var + RW_GN_EPS) * lnw_ref[...] + lnb_ref[...]
    o_ref[...] = ((yn + bonus_ref[...].astype(F32)) * g_ref[...].astype(F32)).astype(o_ref.dtype)


def _post(y, bonus, g, ln_w, ln_b):
    m = y.shape[0]
    tm = min(256, m)
    tok = pl.BlockSpec((tm, D_MODEL), lambda i: (i, 0))
    vec = pl.BlockSpec((1, D_MODEL), lambda i: (0, 0))
    return pl.pallas_call(
        _post_kernel,
        out_shape=jax.ShapeDtypeStruct((m, D_MODEL), BF16),
        grid=(m // tm,),
        in_specs=[tok, tok, tok, vec, vec],
        out_specs=tok,
        compiler_params=_params(1),
        name="rwkv_post",
    )(y, bonus, g, ln_w, ln_b)


def _merge_kernel(oa_ref, ob_ref, ga_ref, gb_ref, ba_ref, bb_ref, wa_ref, wb_ref, wo_ref,
                  x_ref, gpost_ref, o_ref, acc_ref):
    j = pl.program_id(1)

    @pl.when(j == 0)
    def _():
        acc_ref[...] = jnp.zeros(acc_ref.shape, F32)

    gate_a = _sigmoid(ga_ref[...].astype(F32) + ba_ref[...])
    gate_b = _sigmoid(gb_ref[...].astype(F32) + bb_ref[...])
    mixed = gate_a * _dot(oa_ref[...], wa_ref[...]) + gate_b * _dot(ob_ref[...], wb_ref[...])
    acc_ref[...] += _dot(mixed.astype(BF16), wo_ref[...])

    @pl.when(j == pl.num_programs(1) - 1)
    def _():
        o_ref[...] = x_ref[...] + _rms(acc_ref[...], EPS) * gpost_ref[...]


def _merge(o_a, o_b, proj, b_gate, w_a, w_b, w_o, x2, g_post):
    m = x2.shape[0]
    tm = min(512, m)
    tn = 512
    nj = D_MODEL // tn
    gcol = P_GATE // tn
    row = pl.BlockSpec((tm, D_MODEL), lambda i, j: (i, 0))
    return pl.pallas_call(
        _merge_kernel,
        out_shape=jax.ShapeDtypeStruct((m, D_MODEL), F32),
        grid=(m // tm, nj),
        in_specs=[
            row, row,
            pl.BlockSpec((tm, tn), lambda i, j: (i, gcol + j)),
            pl.BlockSpec((tm, tn), lambda i, j: (i, gcol + nj + j)),
            pl.BlockSpec((1, tn), lambda i, j: (0, j)),
            pl.BlockSpec((1, tn), lambda i, j: (0, nj + j)),
            pl.BlockSpec((D_MODEL, tn), lambda i, j: (0, j)),
            pl.BlockSpec((D_MODEL, tn), lambda i, j: (0, j)),
            pl.BlockSpec((tn, D_MODEL), lambda i, j: (j, 0)),
            row,
            pl.BlockSpec((1, D_MODEL), lambda i, j: (0, 0)),
        ],
        out_specs=row,
        scratch_shapes=[pltpu.VMEM((tm, D_MODEL), F32)],
        compiler_params=_params(2),
        name="merge",
    )(o_a, o_b, proj, proj, b_gate, b_gate, w_a, w_b, w_o, x2, g_post)


def _ffn_kernel(x_ref, gpre_ref, wg_ref, wu_ref, wo_ref, gpost_ref, o_ref, hn_ref, acc_ref):
    k = pl.program_id(1)

    @pl.when(k == 0)
    def _():
        hn_ref[...] = (_rms(x_ref[...], EPS) * gpre_ref[...]).astype(BF16)
        acc_ref[...] = jnp.zeros(acc_ref.shape, F32)

    hn = hn_ref[...]
    gate = _dot(hn, wg_ref[...])
    up = _dot(hn, wu_ref[...])
    f = gate * _sigmoid(gate) * up
    acc_ref[...] += _dot(f.astype(BF16), wo_ref[...])

    @pl.when(k == pl.num_programs(1) - 1)
    def _():
        o_ref[...] = x_ref[...] + _rms(acc_ref[...], EPS) * gpost_ref[...]


def _ffn(x1, g_pre, w_in, w_out, g_post):
    m = x1.shape[0]
    tm = min(512, m)
    tf = 512
    nk = D_FF // tf
    row = pl.BlockSpec((tm, D_MODEL), lambda i, k: (i, 0))
    vec = pl.BlockSpec((1, D_MODEL), lambda i, k: (0, 0))
    return pl.pallas_call(
        _ffn_kernel,
        out_shape=jax.ShapeDtypeStruct((m, D_MODEL), F32),
        grid=(m // tm, nk),
        in_specs=[
            row, vec,
            pl.BlockSpec((D_MODEL, tf), lambda i, k: (0, k)),
            pl.BlockSpec((D_MODEL, tf), lambda i, k: (0, nk + k)),
            pl.BlockSpec((tf, D_MODEL), lambda i, k: (k, 0)),
            vec,
        ],
        out_specs=row,
        scratch_shapes=[pltpu.VMEM((tm, D_MODEL), BF16), pltpu.VMEM((tm, D_MODEL), F32)],
        compiler_params=_params(2),
        name="ffn",
    )(x1, g_pre, w_in, w_in, w_out, g_post)


def _pad_lora_cols(w, axis):
    o2, o3 = 3 * D_MODEL, 3 * D_MODEL + RW_DECAY_RANK
    o4 = o3 + RW_AAA_RANK
    take = lambda a, b: lax.slice_in_dim(w, a, b, axis=axis)
    pad_shape = list(w.shape)
    pad_shape[axis] = LORA_PAD - RW_DECAY_RANK
    pad = jnp.zeros(pad_shape, w.dtype)
    return jnp.concatenate([take(0, o2), take(o2, o3), pad, take(o3, o4), pad,
                            take(o4, o4 + RW_GATE_RANK)], axis=axis)


def _pad_rows(w, rows):
    return jnp.concatenate([w, jnp.zeros((rows - w.shape[0], w.shape[1]), w.dtype)], axis=0)


def _rope_tables(seq):
    half = DA_HEAD_DIM // 2
    inv = ROPE_THETA ** (-jnp.arange(half, dtype=F32) / half)
    ang = jnp.arange(seq, dtype=jnp.int32).astype(F32)[:, None] * inv[None, :]
    cos, sin = jnp.cos(ang), jnp.sin(ang)
    return jnp.concatenate([cos, cos], axis=1), jnp.concatenate([-sin, sin], axis=1)


def kernel(x, w_in, b_gate, g_mix_pre, g_mix_post, g_ffn_pre, g_ffn_post,
           da_lambda_q, da_lambda_k, da_subln_g,
           rw_mu, rw_w0, rw_w_up, rw_a0, rw_a_up, rw_g_up, rw_k_k, rw_k_a, rw_r_k,
           rw_ln_w, rw_ln_b, w_branch_a, w_branch_b, w_out, w_ffn_in, w_ffn_out):
    batch, seq, _ = x.shape
    assert w_in.shape[0] == 1 and seq % RW_CHUNK == 0
    m = batch * seq
    x2 = x.reshape(m, D_MODEL)
    row = lambda t: t.reshape(1, -1).astype(F32)

    w = w_in[0]
    off_rw = 3 * D_MODEL
    off_gate = w.shape[1] - 2 * D_MODEL
    w_p = jnp.concatenate([_pad_lora_cols(w[:, off_rw:off_gate], 1), w[:, :off_rw], w[:, off_gate:]],
                          axis=1).astype(BF16)
    cos_t, sin_t = _rope_tables(seq)

    proj = _proj(x2, row(g_mix_pre[0]), w_p, cos_t, sin_t, seq)
    o_a = _attn(proj, da_lambda_q[0], da_lambda_k[0], row(da_subln_g[0]), batch, seq)

    rt, at, bt, kt, bh, kh, v, g, bonus, gl = _prep(
        proj, _pad_lora_cols(row(rw_mu[0]), 1), row(rw_w0[0]),
        _pad_rows(rw_w_up[0], LORA_PAD).astype(BF16), row(rw_a0[0]),
        _pad_rows(rw_a_up[0], LORA_PAD).astype(BF16), rw_g_up[0].astype(BF16),
        row(rw_k_k[0]), row(rw_k_a[0]), row(rw_r_k[0]), seq)
    y = _chunk(rt, at, bt, kt, bh, kh, v, gl, batch, seq)
    o_b = _post(y, bonus, g, row(rw_ln_w[0]), row(rw_ln_b[0]))

    x1 = _merge(o_a, o_b, proj, row(b_gate[0]), w_branch_a[0].astype(BF16),
                w_branch_b[0].astype(BF16), w_out[0].astype(BF16), x2, row(g_mix_post[0]))
    out = _ffn(x1, row(g_ffn_pre[0]), w_ffn_in[0].astype(BF16), w_ffn_out[0].astype(BF16),
               row(g_ffn_post[0]))
    return out.reshape(batch, seq, D_MODEL)
```

```python
import functools
import math

import jax
import jax.numpy as jnp
from jax import lax
from jax.experimental import pallas as pl
from jax.experimental.pallas import tpu as pltpu

F32 = jnp.float32
BF16 = jnp.bfloat16

D_MODEL = 2048
EPS = 1e-6
ROPE_THETA = 10000.0
ATTN_CHUNK = 64
DA_HEADS = 8
DA_HEAD_DIM = 128
DA_V_DIM = 2 * DA_HEAD_DIM
SUBLN_EPS = 1e-5
LAMBDA_INIT = 0.8 - 0.6 * math.exp(-0.3 * 0)
RW_HEAD_DIM = 64
RW_DECAY_RANK = 96
RW_AAA_RANK = 96
RW_GATE_RANK = 256
RW_GN_EPS = 1e-5 * RW_HEAD_DIM
D_FF = 5632

LANES = 128
MXU_DIM = 256
VMEM_LIMIT = 56 * 1024 * 1024

LORA_PAD = LANES
RW_R, RW_K, RW_V = 0, D_MODEL, 2 * D_MODEL
RW_WL = 3 * D_MODEL
RW_AL = RW_WL + LORA_PAD
RW_GL = RW_AL + LORA_PAD
RW_P = RW_GL + RW_GATE_RANK
P_RW = 0
P_Q = P_RW + RW_P
P_K = P_Q + D_MODEL
P_V = P_K + D_MODEL
P_GATE = P_V + D_MODEL
N_PROJ = P_GATE + 2 * D_MODEL

RW_CHUNK = 64
HEADS_PER_GROUP = MXU_DIM // RW_HEAD_DIM
N_GROUPS = D_MODEL // MXU_DIM
NEG_BIG = -1e30


def _params(n_axes):
    return pltpu.CompilerParams(dimension_semantics=("arbitrary",) * n_axes,
                                vmem_limit_bytes=VMEM_LIMIT)


def _dot(a, b):
    return jnp.dot(a, b, preferred_element_type=F32)


def _dot_nt(a, b):
    return lax.dot_general(a, b, (((1,), (1,)), ((), ())), preferred_element_type=F32)


def _dot_tn(a, b):
    return lax.dot_general(a, b, (((0,), (0,)), ((), ())), preferred_element_type=F32)


def _split_bf16(x):
    hi = x.astype(BF16)
    lo = (x - hi.astype(F32)).astype(BF16)
    return hi, lo


def _rms(x, eps):
    return x * lax.rsqrt(jnp.mean(x * x, axis=-1, keepdims=True) + eps)


def _sigmoid(x):
    return 1.0 / (1.0 + jnp.exp(-x))


def _iota(shape, axis):
    return lax.broadcasted_iota(jnp.int32, shape, axis)


def _proj_kernel(x_ref, g_ref, w_ref, cos_ref, sin_ref, o_ref, hn_ref, *, tn, scale):
    j = pl.program_id(1)
    q_lo, k_lo, v_lo = P_Q // tn, P_K // tn, P_V // tn

    @pl.when(j == 0)
    def _():
        hn_ref[...] = (_rms(x_ref[...], EPS) * g_ref[...]).astype(BF16)

    acc = _dot(hn_ref[...], w_ref[...])
    is_rope = (j >= q_lo) & (j < v_lo)

    @pl.when(is_rope)
    def _():
        cos = cos_ref[...]
        sin = sin_ref[...]
        sc = jnp.where(j < k_lo, scale, 1.0).astype(F32)
        for c in range(tn // LANES):
            t = acc[:, c * LANES:(c + 1) * LANES]
            o = (t * cos + pltpu.roll(t, LANES // 2, 1) * sin) * sc
            o_ref[:, c * LANES:(c + 1) * LANES] = o.astype(o_ref.dtype)

    @pl.when(jnp.logical_not(is_rope))
    def _():
        o_ref[...] = acc.astype(o_ref.dtype)


def _proj(x2, g, w_p, cos_t, sin_t, seq):
    m = x2.shape[0]
    tm = min(1024, seq)
    tn = 512
    n_seq_tiles = seq // tm
    return pl.pallas_call(
        functools.partial(_proj_kernel, tn=tn, scale=DA_HEAD_DIM ** -0.5 * math.log2(math.e)),
        out_shape=jax.ShapeDtypeStruct((m, N_PROJ), BF16),
        grid=(m // tm, N_PROJ // tn),
        in_specs=[
            pl.BlockSpec((tm, D_MODEL), lambda i, j: (i, 0)),
            pl.BlockSpec((1, D_MODEL), lambda i, j: (0, 0)),
            pl.BlockSpec((D_MODEL, tn), lambda i, j: (0, j)),
            pl.BlockSpec((tm, LANES), lambda i, j: (i % n_seq_tiles, 0)),
            pl.BlockSpec((tm, LANES), lambda i, j: (i % n_seq_tiles, 0)),
        ],
        out_specs=pl.BlockSpec((tm, tn), lambda i, j: (i, j)),
        scratch_shapes=[pltpu.VMEM((tm, D_MODEL), BF16)],
        compiler_params=_params(2),
        name="proj",
    )(x2, g, w_p, cos_t, sin_t)


def _lanes(x, width):
    return jnp.concatenate([x] * (width // LANES), axis=1)


def _attn_kernel(lq_ref, lk_ref, g_ref, q_ref, k_ref, v_ref, o_ref,
                 m1, l1, a1, m2, l2, a2, *, tq):
    qi = pl.program_id(2)
    stats = ((m1, l1, a1), (m2, l2, a2))
    for m_ref, l_ref, a_ref in stats:
        m_ref[...] = jnp.full(m_ref.shape, NEG_BIG, F32)
        l_ref[...] = jnp.zeros(l_ref.shape, F32)
        a_ref[...] = jnp.zeros(a_ref.shape, F32)

    def block(kb, mask):
        start = pl.multiple_of(kb * tq, tq)
        for c, (m_ref, l_ref, a_ref) in enumerate(stats):
            q = q_ref[:, c * DA_HEAD_DIM:(c + 1) * DA_HEAD_DIM]
            k = k_ref[pl.ds(start, tq), c * DA_HEAD_DIM:(c + 1) * DA_HEAD_DIM]
            s = _dot_nt(q, k)
            if mask is not None:
                s = jnp.where(mask, s, NEG_BIG)
            m_prev = m_ref[...]
            m_next = jnp.maximum(m_prev, jnp.max(s, axis=1, keepdims=True))
            alpha = jnp.exp2(m_prev - m_next)
            p = jnp.exp2(s - _lanes(m_next, tq))
            l_ref[...] = alpha * l_ref[...] + jnp.sum(p, axis=1, keepdims=True)
            a_ref[...] = (_lanes(alpha, DA_V_DIM) * a_ref[...]
                          + _dot(p.astype(BF16), v_ref[pl.ds(start, tq), :]))
            m_ref[...] = m_next

    def full_block(kb, carry):
        block(kb, None)
        return carry

    lax.fori_loop(0, qi, full_block, 0)
    row_chunk = _iota((tq, tq), 0) // ATTN_CHUNK
    col_chunk = _iota((tq, tq), 1) // ATTN_CHUNK
    block(qi, col_chunk <= row_chunk)

    e = jnp.exp(jnp.sum(lq_ref[...] * lk_ref[...], axis=1, keepdims=True))
    lam = e[0:1, :] - e[1:2, :] + LAMBDA_INIT
    o = a1[...] / _lanes(l1[...], DA_V_DIM) - lam * (a2[...] / _lanes(l2[...], DA_V_DIM))
    o = _rms(o, SUBLN_EPS) * g_ref[...] * (1.0 - LAMBDA_INIT)
    o_ref[...] = o.astype(o_ref.dtype)


def _attn(proj, lq, lk, subln_g, batch, seq):
    m = proj.shape[0]
    tq = min(512, seq)
    nq = seq // tq
    qcol, kcol, vcol = P_Q // DA_V_DIM, P_K // DA_V_DIM, P_V // DA_V_DIM
    return pl.pallas_call(
        functools.partial(_attn_kernel, tq=tq),
        out_shape=jax.ShapeDtypeStruct((m, D_MODEL), BF16),
        grid=(batch, DA_HEADS, nq),
        in_specs=[
            pl.BlockSpec((2, DA_HEAD_DIM), lambda b, h, i: (0, 0)),
            pl.BlockSpec((2, DA_HEAD_DIM), lambda b, h, i: (0, 0)),
            pl.BlockSpec((1, DA_V_DIM), lambda b, h, i: (0, 0)),
            pl.BlockSpec((tq, DA_V_DIM), lambda b, h, i: (b * nq + i, qcol + h)),
            pl.BlockSpec((seq, DA_V_DIM), lambda b, h, i: (b, kcol + h)),
            pl.BlockSpec((seq, DA_V_DIM), lambda b, h, i: (b, vcol + h)),
        ],
        out_specs=pl.BlockSpec((tq, DA_V_DIM), lambda b, h, i: (b * nq + i, h)),
        scratch_shapes=[
            pltpu.VMEM((tq, LANES), F32), pltpu.VMEM((tq, LANES), F32), pltpu.VMEM((tq, DA_V_DIM), F32),
            pltpu.VMEM((tq, LANES), F32), pltpu.VMEM((tq, LANES), F32), pltpu.VMEM((tq, DA_V_DIM), F32),
        ],
        compiler_params=_params(3),
        name="attn",
    )(lq, lk, subln_g, proj, proj, proj)


def _head_ones():
    r = _iota((MXU_DIM, MXU_DIM), 0) // RW_HEAD_DIM
    c = _iota((MXU_DIM, MXU_DIM), 1) // RW_HEAD_DIM
    return r == c


def _group_sum(x, ones_bd):
    outs = []
    for gi in range(x.shape[1] // MXU_DIM):
        hi, lo = _split_bf16(x[:, gi * MXU_DIM:(gi + 1) * MXU_DIM])
        outs.append(_dot(hi, ones_bd) + _dot(lo, ones_bd))
    return jnp.concatenate(outs, axis=1)


def _prep_kernel(p_ref, mu_ref, w0_ref, wup_ref, a0_ref, aup_ref, gup_ref, kk_ref, ka_ref, rk_ref,
                 rt_ref, at_ref, bt_ref, kt_ref, bh_ref, kh_ref, v_ref, g_ref, bonus_ref, gl_ref,
                 carry_ref, *, tt, n_seq_tiles):
    i = pl.program_id(0)

    @pl.when(i % n_seq_tiles == 0)
    def _():
        carry_ref[...] = jnp.zeros(carry_ref.shape, F32)

    p = p_ref[...].astype(F32)
    row = _iota(p.shape, 0)
    prev = jnp.where(row == 0, carry_ref[...], pltpu.roll(p, 1, 0))
    carry_ref[...] = p[tt - 1:tt, :]
    xs = p + mu_ref[...] * (prev - p)

    r = xs[:, RW_R:RW_R + D_MODEL]
    k = xs[:, RW_K:RW_K + D_MODEL]
    v = xs[:, RW_V:RW_V + D_MODEL]
    wl = xs[:, RW_WL:RW_WL + LORA_PAD]
    al = xs[:, RW_AL:RW_AL + LORA_PAD]
    gl = xs[:, RW_GL:RW_GL + RW_GATE_RANK]

    z = w0_ref[...] + _dot(jnp.tanh(wl).astype(BF16), wup_ref[...])
    logd = -math.exp(-0.5) * _sigmoid(z)
    alr = _sigmoid(a0_ref[...] + _dot(al.astype(BF16), aup_ref[...]))
    g = _dot(_sigmoid(gl).astype(BF16), gup_ref[...])

    ones_bd = jnp.where(_head_ones(), 1.0, 0.0).astype(BF16)
    kk = k * kk_ref[...]
    kk = kk / jnp.maximum(jnp.sqrt(_group_sum(kk * kk, ones_bd)), 1e-12)
    kmod = k * (1.0 + (alr - 1.0) * ka_ref[...])
    bonus = _group_sum(r * kmod * rk_ref[...], ones_bd) * v
    b = kk * alr

    tr = _iota((tt, tt), 0)
    tc = _iota((tt, tt), 1)
    same = (tr // RW_CHUNK) == (tc // RW_CHUNK)
    tril = jnp.where(same & (tc <= tr), 1.0, 0.0).astype(BF16)
    blk = jnp.where(same, 1.0, 0.0).astype(BF16)
    hi, lo = _split_bf16(logd)
    c = _dot(tril, hi) + _dot(tril, lo)
    c_tot = _dot(blk, hi) + _dot(blk, lo)

    e_neg = jnp.exp(-c)
    e_end = jnp.exp(c_tot - c)
    rt_ref[...] = (r * jnp.exp(c)).astype(BF16)
    at_ref[...] = (-kk * jnp.exp(c - logd)).astype(BF16)
    bt_ref[...] = (b * e_neg).astype(BF16)
    kt_ref[...] = (kmod * e_neg).astype(BF16)
    bh_ref[...] = (b * e_end).astype(BF16)
    kh_ref[...] = (kmod * e_end).astype(BF16)
    v_ref[...] = v.astype(BF16)
    g_ref[...] = g.astype(BF16)
    bonus_ref[...] = bonus.astype(BF16)
    for ci in range(tt // RW_CHUNK):
        last = ci * RW_CHUNK + RW_CHUNK - 1
        gl_ref[ci] = jnp.exp(c[last:last + 1, :])


def _prep(proj, mu, w0, wup, a0, aup, gup, k_k, k_a, r_k, seq):
    m = proj.shape[0]
    tt = min(128, seq)
    tok = lambda i: (i, 0)
    vec = lambda i: (0, 0)
    big = jax.ShapeDtypeStruct((m, D_MODEL), BF16)
    return pl.pallas_call(
        functools.partial(_prep_kernel, tt=tt, n_seq_tiles=seq // tt),
        out_shape=[big] * 9 + [jax.ShapeDtypeStruct((m // RW_CHUNK, 1, D_MODEL), F32)],
        grid=(m // tt,),
        in_specs=[
            pl.BlockSpec((tt, RW_P), lambda i: (i, P_RW // RW_P)),
            pl.BlockSpec((1, RW_P), vec),
            pl.BlockSpec((1, D_MODEL), vec),
            pl.BlockSpec((LORA_PAD, D_MODEL), vec),
            pl.BlockSpec((1, D_MODEL), vec),
            pl.BlockSpec((LORA_PAD, D_MODEL), vec),
            pl.BlockSpec((RW_GATE_RANK, D_MODEL), vec),
            pl.BlockSpec((1, D_MODEL), vec),
            pl.BlockSpec((1, D_MODEL), vec),
            pl.BlockSpec((1, D_MODEL), vec),
        ],
        out_specs=[pl.BlockSpec((tt, D_MODEL), tok)] * 9
        + [pl.BlockSpec((tt // RW_CHUNK, 1, D_MODEL), lambda i: (i, 0, 0))],
        scratch_shapes=[pltpu.VMEM((1, RW_P), F32)],
        compiler_params=_params(1),
        name="rwkv_prep",
    )(proj, mu, w0, wup, a0, aup, gup, k_k, k_a, r_k)


def _stack_heads(x, lane_head):
    zero = jnp.zeros_like(x)
    return jnp.concatenate([jnp.where(lane_head == h, x, zero) for h in range(HEADS_PER_GROUP)], axis=0)


def _chunk_kernel(rt_ref, at_ref, bt_ref, kt_ref, bh_ref, kh_ref, v_ref, gl_ref, y_ref, h_ref):
    L = RW_CHUNK

    @pl.when(pl.program_id(1) == 0)
    def _():
        h_ref[...] = jnp.zeros(h_ref.shape, F32)

    row = _iota((L, MXU_DIM), 0)
    col = _iota((L, MXU_DIM), 1)
    lane_head = col // RW_HEAD_DIM
    src = col % L
    strict = src < row
    incl = src <= row
    eye_packed = jnp.where(src == row, 1.0, 0.0).astype(F32)
    r256 = _iota((MXU_DIM, MXU_DIM), 0)
    c256 = _iota((MXU_DIM, MXU_DIM), 1)
    eye256 = r256 == c256
    same_head = (r256 // RW_HEAD_DIM) == (c256 // RW_HEAD_DIM)
    gamma = gl_ref[0]

    def stack(x):
        return _stack_heads(x.astype(BF16), lane_head)

    groups = range(N_GROUPS)
    sls = [slice(gi * MXU_DIM, (gi + 1) * MXU_DIM) for gi in groups]
    rt = [rt_ref[:, sl] for sl in sls]
    at = [at_ref[:, sl] for sl in sls]
    v = [v_ref[:, sl] for sl in sls]
    sv = [stack(v[g]) for g in groups]

    pair = [_dot_nt(jnp.concatenate([at[g], rt[g]], axis=0),
                    jnp.concatenate([stack(bt_ref[:, sls[g]]), stack(kt_ref[:, sls[g]])], axis=0))
            for g in groups]
    a_ab = [jnp.where(strict, pair[g][:L, :MXU_DIM], 0.0) for g in groups]
    a_ak = [jnp.where(strict, pair[g][:L, MXU_DIM:], 0.0).astype(BF16) for g in groups]
    a_rb = [jnp.where(incl, pair[g][L:, :MXU_DIM], 0.0).astype(BF16) for g in groups]
    a_rk = [jnp.where(incl, pair[g][L:, MXU_DIM:], 0.0).astype(BF16) for g in groups]

    pw = [_dot(a_ab[g].astype(BF16), stack(a_ab[g])) for g in groups]
    t_inv = [eye_packed + a_ab[g] for g in groups]
    v_loc = [_dot(a_ak[g], sv[g]) for g in groups]
    y_loc = [_dot(a_rk[g], sv[g]) for g in groups]
    n_log = int(math.log2(L))
    for rd in range(1, n_log):
        if rd < n_log - 1:
            prod = [_dot(pw[g].astype(BF16), jnp.concatenate([stack(pw[g]), stack(t_inv[g])], axis=1))
                    for g in groups]
            pw = [prod[g][:, :MXU_DIM] for g in groups]
            t_inv = [t_inv[g] + prod[g][:, MXU_DIM:] for g in groups]
        else:
            t_inv = [t_inv[g] + _dot(pw[g].astype(BF16), stack(t_inv[g])) for g in groups]
    sol = [_dot(t_inv[g].astype(BF16), jnp.concatenate([stack(at[g]), stack(v_loc[g])], axis=1))
           for g in groups]
    a_eff = [sol[g][:, :MXU_DIM] for g in groups]
    u_loc = [sol[g][:, MXU_DIM:] for g in groups]
    mix = [_dot(a_rb[g], jnp.concatenate([stack(a_eff[g]), stack(u_loc[g])], axis=1)) for g in groups]
    r_eff = [rt[g].astype(F32) + mix[g][:, :MXU_DIM] for g in groups]
    y_loc = [y_loc[g] + mix[g][:, MXU_DIM:] for g in groups]

    h0 = [h_ref[g] for g in groups]
    from_state = [_dot(jnp.concatenate([a_eff[g], r_eff[g]], axis=0).astype(BF16), h0[g].astype(BF16))
                  for g in groups]
    for g in groups:
        y_ref[:, sls[g]] = from_state[g][L:] + y_loc[g]
    upd = [_dot_tn(jnp.concatenate([bh_ref[:, sls[g]], kh_ref[:, sls[g]]], axis=0),
                   jnp.concatenate([(from_state[g][:L] + u_loc[g]).astype(BF16), v[g]], axis=0))
           for g in groups]
    for g in groups:
        g_row = jnp.broadcast_to(gamma[:, sls[g]], (MXU_DIM, MXU_DIM))
        g_col = jnp.sum(jnp.where(eye256, g_row, 0.0), axis=1, keepdims=True)
        h_ref[g] = g_col * h0[g] + jnp.where(same_head, upd[g], 0.0)


def _chunk(rt, at, bt, kt, bh, kh, v, gl, batch, seq):
    m = rt.shape[0]
    nc = seq // RW_CHUNK
    tok = pl.BlockSpec((RW_CHUNK, D_MODEL), lambda b, c: (b * nc + c, 0))
    return pl.pallas_call(
        _chunk_kernel,
        out_shape=jax.ShapeDtypeStruct((m, D_MODEL), F32),
        grid=(batch, nc),
        in_specs=[tok] * 7 + [pl.BlockSpec((1, 1, D_MODEL), lambda b, c: (b * nc + c, 0, 0))],
        out_specs=tok,
        scratch_shapes=[pltpu.VMEM((N_GROUPS, MXU_DIM, MXU_DIM), F32)],
        compiler_params=_params(2),
        name="rwkv_chunk",
    )(rt, at, bt, kt, bh, kh, v, gl)


def _post_kernel(y_ref, bonus_ref, g_ref, lnw_ref, lnb_ref, o_ref):
    ones_bd = jnp.where(_head_ones(), 1.0, 0.0).astype(BF16)
    y = y_ref[...]
    mean = _group_sum(y, ones_bd) * (1.0 / RW_HEAD_DIM)
    d = y - mean
    var = _group_sum(d * d, ones_bd) * (1.0 / RW_HEAD_DIM)
    yn = d * lax.rsqrt(var + RW_GN_EPS) * lnw_ref[...] + lnb_ref[...]
    o_ref[...] = ((yn + bonus_ref[...].astype(F32)) * g_ref[...].astype(F32)).astype(o_ref.dtype)


def _post(y, bonus, g, ln_w, ln_b):
    m = y.shape[0]
    tm = min(256, m)
    tok = pl.BlockSpec((tm, D_MODEL), lambda i: (i, 0))
    vec = pl.BlockSpec((1, D_MODEL), lambda i: (0, 0))
    return pl.pallas_call(
        _post_kernel,
        out_shape=jax.ShapeDtypeStruct((m, D_MODEL), BF16),
        grid=(m // tm,),
        in_specs=[tok, tok, tok, vec, vec],
        out_specs=tok,
        compiler_params=_params(1),
        name="rwkv_post",
    )(y, bonus, g, ln_w, ln_b)


def _merge_kernel(oa_ref, ob_ref, ga_ref, gb_ref, ba_ref, bb_ref, wa_ref, wb_ref, wo_ref,
                  x_ref, gpost_ref, o_ref, acc_ref):
    j = pl.program_id(1)

    @pl.when(j == 0)
    def _():
        acc_ref[...] = jnp.zeros(acc_ref.shape, F32)

    gate_a = _sigmoid(ga_ref[...].astype(F32) + ba_ref[...])
    gate_b = _sigmoid(gb_ref[...].astype(F32) + bb_ref[...])
    mixed = gate_a * _dot(oa_ref[...], wa_ref[...]) + gate_b * _dot(ob_ref[...], wb_ref[...])
    acc_ref[...] += _dot(mixed.astype(BF16), wo_ref[...])

    @pl.when(j == pl.num_programs(1) - 1)
    def _():
        o_ref[...] = x_ref[...] + _rms(acc_ref[...], EPS) * gpost_ref[...]


def _merge(o_a, o_b, proj, b_gate, w_a, w_b, w_o, x2, g_post):
    m = x2.shape[0]
    tm = min(512, m)
    tn = 512
    nj = D_MODEL // tn
    gcol = P_GATE // tn
    row = pl.BlockSpec((tm, D_MODEL), lambda i, j: (i, 0))
    return pl.pallas_call(
        _merge_kernel,
        out_shape=jax.ShapeDtypeStruct((m, D_MODEL), F32),
        grid=(m // tm, nj),
        in_specs=[
            row, row,
            pl.BlockSpec((tm, tn), lambda i, j: (i, gcol + j)),
            pl.BlockSpec((tm, tn), lambda i, j: (i, gcol + nj + j)),
            pl.BlockSpec((1, tn), lambda i, j: (0, j)),
            pl.BlockSpec((1, tn), lambda i, j: (0, nj + j)),
            pl.BlockSpec((D_MODEL, tn), lambda i, j: (0, j)),
            pl.BlockSpec((D_MODEL, tn), lambda i, j: (0, j)),
            pl.BlockSpec((tn, D_MODEL), lambda i, j: (j, 0)),
            row,
            pl.BlockSpec((1, D_MODEL), lambda i, j: (0, 0)),
        ],
        out_specs=row,
        scratch_shapes=[pltpu.VMEM((tm, D_MODEL), F32)],
        compiler_params=_params(2),
        name="merge",
    )(o_a, o_b, proj, proj, b_gate, b_gate, w_a, w_b, w_o, x2, g_post)


def _ffn_kernel(x_ref, gpre_ref, wg_ref, wu_ref, wo_ref, gpost_ref, o_ref, hn_ref, acc_ref):
    k = pl.program_id(1)

    @pl.when(k == 0)
    def _():
        hn_ref[...] = (_rms(x_ref[...], EPS) * gpre_ref[...]).astype(BF16)
        acc_ref[...] = jnp.zeros(acc_ref.shape, F32)

    hn = hn_ref[...]
    gate = _dot(hn, wg_ref[...])
    up = _dot(hn, wu_ref[...])
    f = gate * _sigmoid(gate) * up
    acc_ref[...] += _dot(f.astype(BF16), wo_ref[...])

    @pl.when(k == pl.num_programs(1) - 1)
    def _():
        o_ref[...] = x_ref[...] + _rms(acc_ref[...], EPS) * gpost_ref[...]


def _ffn(x1, g_pre, w_in, w_out, g_post):
    m = x1.shape[0]
    tm = min(512, m)
    tf = 512
    nk = D_FF // tf
    row = pl.BlockSpec((tm, D_MODEL), lambda i, k: (i, 0))
    vec = pl.BlockSpec((1, D_MODEL), lambda i, k: (0, 0))
    return pl.pallas_call(
        _ffn_kernel,
        out_shape=jax.ShapeDtypeStruct((m, D_MODEL), F32),
        grid=(m // tm, nk),
        in_specs=[
            row, vec,
            pl.BlockSpec((D_MODEL, tf), lambda i, k: (0, k)),
            pl.BlockSpec((D_MODEL, tf), lambda i, k: (0, nk + k)),
            pl.BlockSpec((tf, D_MODEL), lambda i, k: (k, 0)),
            vec,
        ],
        out_specs=row,
        scratch_shapes=[pltpu.VMEM((tm, D_MODEL), BF16), pltpu.VMEM((tm, D_MODEL), F32)],
        compiler_params=_params(2),
        name="ffn",
    )(x1, g_pre, w_in, w_in, w_out, g_post)


def _pad_lora_cols(w, axis):
    o2, o3 = 3 * D_MODEL, 3 * D_MODEL + RW_DECAY_RANK
    o4 = o3 + RW_AAA_RANK
    take = lambda a, b: lax.slice_in_dim(w, a, b, axis=axis)
    pad_shape = list(w.shape)
    pad_shape[axis] = LORA_PAD - RW_DECAY_RANK
    pad = jnp.zeros(pad_shape, w.dtype)
    return jnp.concatenate([take(0, o2), take(o2, o3), pad, take(o3, o4), pad,
                            take(o4, o4 + RW_GATE_RANK)], axis=axis)


def _pad_rows(w, rows):
    return jnp.concatenate([w, jnp.zeros((rows - w.shape[0], w.shape[1]), w.dtype)], axis=0)


def _rope_tables(seq):
    half = DA_HEAD_DIM // 2
    inv = ROPE_THETA ** (-jnp.arange(half, dtype=F32) / half)
    ang = jnp.arange(seq, dtype=jnp.int32).astype(F32)[:, None] * inv[None, :]
    cos, sin = jnp.cos(ang), jnp.sin(ang)
    return jnp.concatenate([cos, cos], axis=1), jnp.concatenate([-sin, sin], axis=1)


def kernel(x, w_in, b_gate, g_mix_pre, g_mix_post, g_ffn_pre, g_ffn_post,
           da_lambda_q, da_lambda_k, da_subln_g,
           rw_mu, rw_w0, rw_w_up, rw_a0, rw_a_up, rw_g_up, rw_k_k, rw_k_a, rw_r_k,
           rw_ln_w, rw_ln_b, w_branch_a, w_branch_b, w_out, w_ffn_in, w_ffn_out):
    batch, seq, _ = x.shape
    assert w_in.shape[0] == 1 and seq % RW_CHUNK == 0
    m = batch * seq
    x2 = x.reshape(m, D_MODEL)
    row = lambda t: t.reshape(1, -1).astype(F32)

    w = w_in[0]
    off_rw = 3 * D_MODEL
    off_gate = w.shape[1] - 2 * D_MODEL
    w_p = jnp.concatenate([_pad_lora_cols(w[:, off_rw:off_gate], 1), w[:, :off_rw], w[:, off_gate:]],
                          axis=1).astype(BF16)
    cos_t, sin_t = _rope_tables(seq)

    proj = _proj(x2, row(g_mix_pre[0]), w_p, cos_t, sin_t, seq)
    o_a = _attn(proj, da_lambda_q[0], da_lambda_k[0], row(da_subln_g[0]), batch, seq)

    rt, at, bt, kt, bh, kh, v, g, bonus, gl = _prep(
        proj, _pad_lora_cols(row(rw_mu[0]), 1), row(rw_w0[0]),
        _pad_rows(rw_w_up[0], LORA_PAD).astype(BF16), row(rw_a0[0]),
        _pad_rows(rw_a_up[0], LORA_PAD).astype(BF16), rw_g_up[0].astype(BF16),
        row(rw_k_k[0]), row(rw_k_a[0]), row(rw_r_k[0]), seq)
    y = _chunk(rt, at, bt, kt, bh, kh, v, gl, batch, seq)
    o_b = _post(y, bonus, g, row(rw_ln_w[0]), row(rw_ln_b[0]))

    x1 = _merge(o_a, o_b, proj, row(b_gate[0]), w_branch_a[0].astype(BF16),
                w_branch_b[0].astype(BF16), w_out[0].astype(BF16), x2, row(g_mix_post[0]))
    out = _ffn(x1, row(g_ffn_pre[0]), w_ffn_in[0].astype(BF16), w_ffn_out[0].astype(BF16),
               row(g_ffn_post[0]))
    return out.reshape(batch, seq, D_MODEL)
```

```python
import functools
import math

import jax
import jax.numpy as jnp
from jax import lax
from jax.experimental import pallas as pl
from jax.experimental.pallas import tpu as pltpu

F32 = jnp.float32
BF16 = jnp.bfloat16

D_MODEL = 2048
EPS = 1e-6
ROPE_THETA = 10000.0
ATTN_CHUNK = 64
DA_HEADS = 8
DA_HEAD_DIM = 128
DA_V_DIM = 2 * DA_HEAD_DIM
SUBLN_EPS = 1e-5
LAMBDA_INIT = 0.8 - 0.6 * math.exp(-0.3 * 0)
RW_HEAD_DIM = 64
RW_DECAY_RANK = 96
RW_AAA_RANK = 96
RW_GATE_RANK = 256
RW_GN_EPS = 1e-5 * RW_HEAD_DIM
D_FF = 5632

LANES = 128
MXU_DIM = 256
VMEM_LIMIT = 56 * 1024 * 1024

N_MAIN = 6 * D_MODEL
M_Q, M_K, M_V, M_RR, M_RK, M_RV = (i * D_MODEL for i in range(6))
LORA_PAD = LANES
X_WL = 0
X_AL = X_WL + LORA_PAD
X_GL = X_AL + LORA_PAD
X_LORA = X_GL + RW_GATE_RANK
X_GATE = X_LORA
N_REST = X_GATE + 2 * D_MODEL

RW_CHUNK = 64
HEADS_PER_GROUP = MXU_DIM // RW_HEAD_DIM
N_GROUPS = D_MODEL // MXU_DIM
NEG_BIG = -1e30


def _params(n_axes, flags=None):
    return pltpu.CompilerParams(dimension_semantics=("arbitrary",) * n_axes,
                                vmem_limit_bytes=VMEM_LIMIT, flags=flags)


def _dot(a, b):
    return jnp.dot(a, b, preferred_element_type=F32)


def _dot_nt(a, b):
    return lax.dot_general(a, b, (((1,), (1,)), ((), ())), preferred_element_type=F32)


def _dot_tn(a, b):
    return lax.dot_general(a, b, (((0,), (0,)), ((), ())), preferred_element_type=F32)


def _split_bf16(x):
    hi = x.astype(BF16)
    lo = (x - hi.astype(F32)).astype(BF16)
    return hi, lo


def _rms(x, eps):
    return x * lax.rsqrt(jnp.mean(x * x, axis=-1, keepdims=True) + eps)


def _sigmoid(x):
    return 1.0 / (1.0 + jnp.exp(-x))


def _iota(shape, axis):
    return lax.broadcasted_iota(jnp.int32, shape, axis)


def _proj_main_kernel(x_ref, g_ref, w_ref, cos_ref, sin_ref, o_ref, hn_ref, *, tn, scale):
    j = pl.program_id(1)
    k_lo, v_lo = M_K // tn, M_V // tn

    @pl.when(j == 0)
    def _():
        hn_ref[...] = (_rms(x_ref[...], EPS) * g_ref[...]).astype(BF16)

    acc = _dot(hn_ref[...], w_ref[...])

    @pl.when(j < v_lo)
    def _():
        cos = cos_ref[...]
        sin = sin_ref[...]
        sc = jnp.where(j < k_lo, scale, 1.0).astype(F32)
        for c in range(tn // LANES):
            t = acc[:, c * LANES:(c + 1) * LANES]
            o = (t * cos + pltpu.roll(t, LANES // 2, 1) * sin) * sc
            o_ref[:, c * LANES:(c + 1) * LANES] = o.astype(o_ref.dtype)

    @pl.when(j >= v_lo)
    def _():
        o_ref[...] = acc.astype(o_ref.dtype)


def _proj_main(x2, g, w_bf, cos_t, sin_t, seq):
    m = x2.shape[0]
    tm = min(1024, seq)
    tn = 1024
    n_seq_tiles = seq // tm
    return pl.pallas_call(
        functools.partial(_proj_main_kernel, tn=tn, scale=DA_HEAD_DIM ** -0.5 * math.log2(math.e)),
        out_shape=jax.ShapeDtypeStruct((m, N_MAIN), BF16),
        grid=(m // tm, N_MAIN // tn),
        in_specs=[
            pl.BlockSpec((tm, D_MODEL), lambda i, j: (i, 0)),
            pl.BlockSpec((1, D_MODEL), lambda i, j: (0, 0)),
            pl.BlockSpec((D_MODEL, tn), lambda i, j: (0, j)),
            pl.BlockSpec((tm, LANES), lambda i, j: (i % n_seq_tiles, 0)),
            pl.BlockSpec((tm, LANES), lambda i, j: (i % n_seq_tiles, 0)),
        ],
        out_specs=pl.BlockSpec((tm, tn), lambda i, j: (i, j)),
        scratch_shapes=[pltpu.VMEM((tm, D_MODEL), BF16)],
        compiler_params=_params(2),
        name="proj_main",
    )(x2, g, w_bf, cos_t, sin_t)


def _proj_rest_kernel(x_ref, g_ref, w_ref, o_ref, hn_ref):
    @pl.when(pl.program_id(1) == 0)
    def _():
        hn_ref[...] = (_rms(x_ref[...], EPS) * g_ref[...]).astype(BF16)

    o_ref[...] = _dot(hn_ref[...], w_ref[...]).astype(o_ref.dtype)


def _proj_rest(x2, g, w_rest):
    m = x2.shape[0]
    tm = min(1024, m)
    tn = N_REST // 3
    return pl.pallas_call(
        _proj_rest_kernel,
        out_shape=jax.ShapeDtypeStruct((m, N_REST), BF16),
        grid=(m // tm, N_REST // tn),
        in_specs=[
            pl.BlockSpec((tm, D_MODEL), lambda i, j: (i, 0)),
            pl.BlockSpec((1, D_MODEL), lambda i, j: (0, 0)),
            pl.BlockSpec((D_MODEL, tn), lambda i, j: (0, j)),
        ],
        out_specs=pl.BlockSpec((tm, tn), lambda i, j: (i, j)),
        scratch_shapes=[pltpu.VMEM((tm, D_MODEL), BF16)],
        compiler_params=_params(2),
        name="proj_rest",
    )(x2, g, w_rest)


def _lanes(x, width):
    return jnp.concatenate([x] * (width // LANES), axis=1)


def _attn_kernel(lq_ref, lk_ref, g_ref, q_ref, k_ref, v_ref, o_ref,
                 m1, l1, a1, m2, l2, a2, s_ref, bmax_ref, p_ref, *, tq, tk):
    qi = pl.program_id(2)
    stats = ((m1, l1, a1), (m2, l2, a2))
    for m_ref, l_ref, a_ref in stats:
        m_ref[...] = jnp.full(m_ref.shape, NEG_BIG, F32)
        l_ref[...] = jnp.zeros(l_ref.shape, F32)
        a_ref[...] = jnp.zeros(a_ref.shape, F32)

    def scores(kb, slot, cols=slice(None)):
        start = pl.multiple_of(kb * tk, tk)
        for c in range(2):
            s = _dot_nt(k_ref[pl.ds(start, tk), c * DA_HEAD_DIM:(c + 1) * DA_HEAD_DIM],
                        q_ref[cols, c * DA_HEAD_DIM:(c + 1) * DA_HEAD_DIM])
            s_ref[slot, c, :, cols] = s
            bmax_ref[slot, c, :, cols] = jnp.max(s, axis=0, keepdims=True)

    def accumulate(kb, slot, mask=None, cols=slice(None)):
        start = pl.multiple_of(kb * tk, tk)
        v_t = v_ref[pl.ds(start, tk), :].T
        for c, (m_ref, l_ref, a_ref) in enumerate(stats):
            s = s_ref[slot, c, :, cols]
            if mask is None:
                block_max = bmax_ref[slot, c, :, cols]
            else:
                s = jnp.where(mask, s, NEG_BIG)
                block_max = jnp.max(s, axis=0, keepdims=True)
            m_prev = m_ref[:, cols]
            m_next = jnp.maximum(m_prev, block_max)
            alpha = jnp.exp2(m_prev - m_next)
            p = jnp.exp2(s - m_next)
            l_ref[:, cols] = alpha * l_ref[:, cols] + jnp.sum(p, axis=0, keepdims=True)
            p_ref[c, :, cols] = p.astype(BF16)
            a_ref[:, cols] = alpha * a_ref[:, cols] + _dot(v_t, p_ref[c, :, cols])
            m_ref[:, cols] = m_next

    def pipelined(kp, carry):
        kb = 2 * kp
        scores(kb + 1, 1)
        accumulate(kb, 0)
        scores(kb + 2, 0)
        accumulate(kb + 1, 1)
        return carry

    scores(0, 0)
    lax.fori_loop(0, qi, pipelined, 0)
    first, last = slice(0, tk), slice(tk, tq)
    diag_mask = (_iota((tk, tk), 0) // ATTN_CHUNK) <= (_iota((tk, tk), 1) // ATTN_CHUNK)
    scores(2 * qi + 1, 1, last)
    accumulate(2 * qi, 0, diag_mask, first)
    accumulate(2 * qi, 0, None, last)
    accumulate(2 * qi + 1, 1, diag_mask, last)

    e = jnp.exp(jnp.sum(lq_ref[...] * lk_ref[...], axis=1, keepdims=True))
    lam = e[0:1, :] - e[1:2, :] + LAMBDA_INIT
    o_t = a1[...] / l1[...] - lam * (a2[...] / l2[...])
    o_t = o_t * lax.rsqrt(jnp.mean(o_t * o_t, axis=0, keepdims=True) + SUBLN_EPS)
    o_ref[...] = (o_t.T * g_ref[...] * (1.0 - LAMBDA_INIT)).astype(o_ref.dtype)


def _attn(proj, lq, lk, subln_g, batch, seq):
    m = proj.shape[0]
    tq = min(1024, seq)
    tk = tq // 2
    nq = seq // tq
    qcol, kcol, vcol = M_Q // DA_V_DIM, M_K // DA_V_DIM, M_V // DA_V_DIM
    return pl.pallas_call(
        functools.partial(_attn_kernel, tq=tq, tk=tk),
        out_shape=jax.ShapeDtypeStruct((m, D_MODEL), BF16),
        grid=(batch, DA_HEADS, nq),
        in_specs=[
            pl.BlockSpec((2, DA_HEAD_DIM), lambda b, h, i: (0, 0)),
            pl.BlockSpec((2, DA_HEAD_DIM), lambda b, h, i: (0, 0)),
            pl.BlockSpec((1, DA_V_DIM), lambda b, h, i: (0, 0)),
            pl.BlockSpec((tq, DA_V_DIM), lambda b, h, i: (b * nq + i, qcol + h)),
            pl.BlockSpec((seq, DA_V_DIM), lambda b, h, i: (b, kcol + h)),
            pl.BlockSpec((seq, DA_V_DIM), lambda b, h, i: (b, vcol + h)),
        ],
        out_specs=pl.BlockSpec((tq, DA_V_DIM), lambda b, h, i: (b * nq + i, h)),
        scratch_shapes=[
            pltpu.VMEM((1, tq), F32), pltpu.VMEM((1, tq), F32), pltpu.VMEM((DA_V_DIM, tq), F32),
            pltpu.VMEM((1, tq), F32), pltpu.VMEM((1, tq), F32), pltpu.VMEM((DA_V_DIM, tq), F32),
            pltpu.VMEM((2, 2, tk, tq), F32),
            pltpu.VMEM((2, 2, 1, tq), F32),
            pltpu.VMEM((2, tk, tq), BF16),
        ],
        compiler_params=_params(3),
        name="attn",
    )(lq, lk, subln_g, proj, proj, proj)


def _head_ones():
    r = _iota((MXU_DIM, MXU_DIM), 0) // RW_HEAD_DIM
    c = _iota((MXU_DIM, MXU_DIM), 1) // RW_HEAD_DIM
    return r == c


def _group_sum(x, ones_bd):
    outs = [_dot(x[:, gi * MXU_DIM:(gi + 1) * MXU_DIM].astype(BF16), ones_bd)
            for gi in range(x.shape[1] // MXU_DIM)]
    return jnp.concatenate(outs, axis=1)


def _prep_kernel(pr_ref, pk_ref, pv_ref, pl_ref, mu_ref, w0_ref, wup_ref, a0_ref, aup_ref, gup_ref,
                 kk_ref, ka_ref, rk_ref,
                 rt_ref, at_ref, bt_ref, kt_ref, bh_ref, kh_ref, v_ref, g_ref, bonus_ref, gl_ref,
                 carry_ref, *, tt, n_seq_tiles):
    i = pl.program_id(0)

    @pl.when(i % n_seq_tiles == 0)
    def _():
        carry_ref[...] = jnp.zeros(carry_ref.shape, F32)

    p = jnp.concatenate([pr_ref[...], pk_ref[...], pv_ref[...], pl_ref[...]], axis=1).astype(F32)
    row = _iota(p.shape, 0)
    prev = jnp.where(row == 0, carry_ref[...], pltpu.roll(p, 1, 0))
    carry_ref[...] = p[tt - 1:tt, :]
    xs = p + mu_ref[...] * (prev - p)

    lora = 3 * D_MODEL
    r = xs[:, 0:D_MODEL]
    k = xs[:, D_MODEL:2 * D_MODEL]
    v = xs[:, 2 * D_MODEL:3 * D_MODEL]
    wl = xs[:, lora + X_WL:lora + X_WL + LORA_PAD]
    al = xs[:, lora + X_AL:lora + X_AL + LORA_PAD]
    gl = xs[:, lora + X_GL:lora + X_GL + RW_GATE_RANK]

    z = w0_ref[...] + _dot(jnp.tanh(wl).astype(BF16), wup_ref[...])
    logd = -(math.exp(-0.5) * math.log2(math.e)) * _sigmoid(z)
    alr = _sigmoid(a0_ref[...] + _dot(al.astype(BF16), aup_ref[...]))
    g = _dot(_sigmoid(gl).astype(BF16), gup_ref[...])

    ones_bd = jnp.where(_head_ones(), 1.0, 0.0).astype(BF16)
    kk = k * kk_ref[...]
    kk = kk / jnp.maximum(jnp.sqrt(_group_sum(kk * kk, ones_bd)), 1e-12)
    kmod = k * (1.0 + (alr - 1.0) * ka_ref[...])
    bonus = _group_sum(r * kmod * rk_ref[...], ones_bd) * v
    b = kk * alr

    tr = _iota((tt, tt), 0)
    tc = _iota((tt, tt), 1)
    same = (tr // RW_CHUNK) == (tc // RW_CHUNK)
    tril = jnp.where(same & (tc <= tr), 1.0, 0.0).astype(BF16)
    blk = jnp.where(same, 1.0, 0.0).astype(BF16)
    hi, lo = _split_bf16(logd)
    c = _dot(tril, hi) + _dot(tril, lo)
    c_tot = _dot(blk, hi) + _dot(blk, lo)

    e_neg = jnp.exp2(-c)
    e_end = jnp.exp2(c_tot - c)
    rt_ref[...] = (r * jnp.exp2(c)).astype(BF16)
    at_ref[...] = (-kk * jnp.exp2(c - logd)).astype(BF16)
    bt_ref[...] = (b * e_neg).astype(BF16)
    kt_ref[...] = (kmod * e_neg).astype(BF16)
    bh_ref[...] = (b * e_end).astype(BF16)
    kh_ref[...] = (kmod * e_end).astype(BF16)
    v_ref[...] = v.astype(BF16)
    g_ref[...] = g.astype(BF16)
    bonus_ref[...] = bonus.astype(BF16)
    for ci in range(tt // RW_CHUNK):
        last = ci * RW_CHUNK + RW_CHUNK - 1
        gl_ref[ci] = jnp.exp2(c[last:last + 1, :])


def _prep(pm, px, mu, w0, wup, a0, aup, gup, k_k, k_a, r_k, seq):
    m = pm.shape[0]
    tt = min(128, seq)
    tok = lambda i: (i, 0)
    vec = lambda i: (0, 0)
    big = jax.ShapeDtypeStruct((m, D_MODEL), BF16)
    rw_width = 3 * D_MODEL + X_LORA
    return pl.pallas_call(
        functools.partial(_prep_kernel, tt=tt, n_seq_tiles=seq // tt),
        out_shape=[big] * 9 + [jax.ShapeDtypeStruct((m // RW_CHUNK, 1, D_MODEL), F32)],
        grid=(m // tt,),
        in_specs=[
            pl.BlockSpec((tt, D_MODEL), lambda i: (i, M_RR // D_MODEL)),
            pl.BlockSpec((tt, D_MODEL), lambda i: (i, M_RK // D_MODEL)),
            pl.BlockSpec((tt, D_MODEL), lambda i: (i, M_RV // D_MODEL)),
            pl.BlockSpec((tt, X_LORA), lambda i: (i, 0)),
            pl.BlockSpec((1, rw_width), vec),
            pl.BlockSpec((1, D_MODEL), vec),
            pl.BlockSpec((LORA_PAD, D_MODEL), vec),
            pl.BlockSpec((1, D_MODEL), vec),
            pl.BlockSpec((LORA_PAD, D_MODEL), vec),
            pl.BlockSpec((RW_GATE_RANK, D_MODEL), vec),
            pl.BlockSpec((1, D_MODEL), vec),
            pl.BlockSpec((1, D_MODEL), vec),
            pl.BlockSpec((1, D_MODEL), vec),
        ],
        out_specs=[pl.BlockSpec((tt, D_MODEL), tok)] * 9
        + [pl.BlockSpec((tt // RW_CHUNK, 1, D_MODEL), lambda i: (i, 0, 0))],
        scratch_shapes=[pltpu.VMEM((1, rw_width), F32)],
        compiler_params=_params(1),
        name="rwkv_prep",
    )(pm, pm, pm, px, mu, w0, wup, a0, aup, gup, k_k, k_a, r_k)


def _stack_heads(x, lane_head):
    zero = jnp.zeros_like(x)
    return jnp.concatenate([jnp.where(lane_head == h, x, zero) for h in range(HEADS_PER_GROUP)], axis=0)


def _chunk_kernel(rt_ref, at_ref, bt_ref, kt_ref, bh_ref, kh_ref, v_ref, gl_ref, y_ref, h_ref):
    L = RW_CHUNK

    @pl.when(pl.program_id(1) == 0)
    def _():
        h_ref[...] = jnp.zeros(h_ref.shape, F32)

    row = _iota((L, MXU_DIM), 0)
    col = _iota((L, MXU_DIM), 1)
    lane_head = col // RW_HEAD_DIM
    src = col % L
    strict = src < row
    incl = src <= row
    eye_packed = jnp.where(src == row, 1.0, 0.0).astype(F32)
    r256 = _iota((MXU_DIM, MXU_DIM), 0)
    c256 = _iota((MXU_DIM, MXU_DIM), 1)
    eye256 = r256 == c256
    same_head = (r256 // RW_HEAD_DIM) == (c256 // RW_HEAD_DIM)
    gamma = gl_ref[0]

    def stack(x):
        return _stack_heads(x.astype(BF16), lane_head)

    groups = range(N_GROUPS)
    sls = [slice(gi * MXU_DIM, (gi + 1) * MXU_DIM) for gi in groups]
    rt = [rt_ref[:, sl] for sl in sls]
    at = [at_ref[:, sl] for sl in sls]
    v = [v_ref[:, sl] for sl in sls]
    sv = [stack(v[g]) for g in groups]

    pair = [_dot_nt(jnp.concatenate([at[g], rt[g]], axis=0),
                    jnp.concatenate([stack(bt_ref[:, sls[g]]), stack(kt_ref[:, sls[g]])], axis=0))
            for g in groups]
    a_ab = [jnp.where(strict, pair[g][:L, :MXU_DIM], 0.0) for g in groups]
    a_ak = [jnp.where(strict, pair[g][:L, MXU_DIM:], 0.0).astype(BF16) for g in groups]
    a_rb = [jnp.where(incl, pair[g][L:, :MXU_DIM], 0.0).astype(BF16) for g in groups]
    a_rk = [jnp.where(incl, pair[g][L:, MXU_DIM:], 0.0).astype(BF16) for g in groups]

    pw = [_dot(a_ab[g].astype(BF16), stack(a_ab[g])) for g in groups]
    t_inv = [eye_packed + a_ab[g] for g in groups]
    on_v = [_dot(jnp.concatenate([a_ak[g], a_rk[g]], axis=0), sv[g]) for g in groups]
    v_loc = [on_v[g][:L] for g in groups]
    y_loc = [on_v[g][L:] for g in groups]
    n_log = int(math.log2(L))
    for rd in range(1, n_log):
        if rd < n_log - 1:
            prod = [_dot(pw[g].astype(BF16), jnp.concatenate([stack(pw[g]), stack(t_inv[g])], axis=1))
                    for g in groups]
            pw = [prod[g][:, :MXU_DIM] for g in groups]
            t_inv = [t_inv[g] + prod[g][:, MXU_DIM:] for g in groups]
        else:
            t_inv = [t_inv[g] + _dot(pw[g].astype(BF16), stack(t_inv[g])) for g in groups]
    sol = [_dot(t_inv[g].astype(BF16), jnp.concatenate([stack(at[g]), stack(v_loc[g])], axis=1))
           for g in groups]
    a_eff = [sol[g][:, :MXU_DIM] for g in groups]
    u_loc = [sol[g][:, MXU_DIM:] for g in groups]
    mix = [_dot(a_rb[g], jnp.concatenate([stack(a_eff[g]), stack(u_loc[g])], axis=1)) for g in groups]
    r_eff = [rt[g].astype(F32) + mix[g][:, :MXU_DIM] for g in groups]
    y_loc = [y_loc[g] + mix[g][:, MXU_DIM:] for g in groups]

    h0 = [h_ref[g] for g in groups]
    from_state = [_dot(jnp.concatenate([a_eff[g], r_eff[g]], axis=0).astype(BF16), h0[g].astype(BF16))
                  for g in groups]
    for g in groups:
        y_ref[:, sls[g]] = from_state[g][L:] + y_loc[g]
    upd = [_dot_tn(jnp.concatenate([bh_ref[:, sls[g]], kh_ref[:, sls[g]]], axis=0),
                   jnp.concatenate([(from_state[g][:L] + u_loc[g]).astype(BF16), v[g]], axis=0))
           for g in groups]
    for g in groups:
        g_row = jnp.broadcast_to(gamma[:, sls[g]], (MXU_DIM, MXU_DIM))
        g_col = jnp.sum(jnp.where(eye256, g_row, 0.0), axis=1, keepdims=True)
        h_ref[g] = g_col * h0[g] + jnp.where(same_head, upd[g], 0.0)


def _chunk(rt, at, bt, kt, bh, kh, v, gl, batch, seq):
    m = rt.shape[0]
    nc = seq // RW_CHUNK
    tok = pl.BlockSpec((RW_CHUNK, D_MODEL), lambda b, c: (b * nc + c, 0))
    return pl.pallas_call(
        _chunk_kernel,
        out_shape=jax.ShapeDtypeStruct((m, D_MODEL), F32),
        grid=(batch, nc),
        in_specs=[tok] * 7 + [pl.BlockSpec((1, 1, D_MODEL), lambda b, c: (b * nc + c, 0, 0))],
        out_specs=tok,
        scratch_shapes=[pltpu.VMEM((N_GROUPS, MXU_DIM, MXU_DIM), F32)],
        compiler_params=_params(2),
        name="rwkv_chunk",
    )(rt, at, bt, kt, bh, kh, v, gl)


def _post_kernel(y_ref, bonus_ref, g_ref, lnw_ref, lnb_ref, o_ref):
    ones_bd = jnp.where(_head_ones(), 1.0, 0.0).astype(BF16)
    y = y_ref[...]
    mean = _group_sum(y, ones_bd) * (1.0 / RW_HEAD_DIM)
    d = y - mean
    var = _group_sum(d * d, ones_bd) * (1.0 / RW_HEAD_DIM)
    yn = d * lax.rsqrt(var + RW_GN_EPS) * lnw_ref[...] + lnb_ref[...]
    o_ref[...] = ((yn + bonus_ref[...].astype(F32)) * g_ref[...].astype(F32)).astype(o_ref.dtype)


def _post(y, bonus, g, ln_w, ln_b):
    m = y.shape[0]
    tm = min(256, m)
    tok = pl.BlockSpec((tm, D_MODEL), lambda i: (i, 0))
    vec = pl.BlockSpec((1, D_MODEL), lambda i: (0, 0))
    return pl.pallas_call(
        _post_kernel,
        out_shape=jax.ShapeDtypeStruct((m, D_MODEL), BF16),
        grid=(m // tm,),
        in_specs=[tok, tok, tok, vec, vec],
        out_specs=tok,
        compiler_params=_params(1),
        name="rwkv_post",
    )(y, bonus, g, ln_w, ln_b)


def _merge_kernel(oa_ref, ob_ref, ga_ref, gb_ref, ba_ref, bb_ref, wa_ref, wb_ref, wo_ref,
                  x_ref, gpost_ref, o_ref, acc_ref):
    j = pl.program_id(1)

    @pl.when(j == 0)
    def _():
        acc_ref[...] = jnp.zeros(acc_ref.shape, F32)

    gate_a = _sigmoid(ga_ref[...].astype(F32) + ba_ref[...])
    gate_b = _sigmoid(gb_ref[...].astype(F32) + bb_ref[...])
    mixed = gate_a * _dot(oa_ref[...], wa_ref[...]) + gate_b * _dot(ob_ref[...], wb_ref[...])
    acc_ref[...] += _dot(mixed.astype(BF16), wo_ref[...])

    @pl.when(j == pl.num_programs(1) - 1)
    def _():
        o_ref[...] = x_ref[...] + _rms(acc_ref[...], EPS) * gpost_ref[...]


def _merge(o_a, o_b, proj, b_gate, w_a, w_b, w_o, x2, g_post):
    m = x2.shape[0]
    tm = min(512, m)
    tn = 512
    nj = D_MODEL // tn
    gcol = X_GATE // tn
    row = pl.BlockSpec((tm, D_MODEL), lambda i, j: (i, 0))
    return pl.pallas_call(
        _merge_kernel,
        out_shape=jax.ShapeDtypeStruct((m, D_MODEL), F32),
        grid=(m // tm, nj),
        in_specs=[
            row, row,
            pl.BlockSpec((tm, tn), lambda i, j: (i, gcol + j)),
            pl.BlockSpec((tm, tn), lambda i, j: (i, gcol + nj + j)),
            pl.BlockSpec((1, tn), lambda i, j: (0, j)),
            pl.BlockSpec((1, tn), lambda i, j: (0, nj + j)),
            pl.BlockSpec((D_MODEL, tn), lambda i, j: (0, j)),
            pl.BlockSpec((D_MODEL, tn), lambda i, j: (0, j)),
            pl.BlockSpec((tn, D_MODEL), lambda i, j: (j, 0)),
            row,
            pl.BlockSpec((1, D_MODEL), lambda i, j: (0, 0)),
        ],
        out_specs=row,
        scratch_shapes=[pltpu.VMEM((tm, D_MODEL), F32)],
        compiler_params=_params(2),
        name="merge",
    )(o_a, o_b, proj, proj, b_gate, b_gate, w_a, w_b, w_o, x2, g_post)


def _ffn_kernel(x_ref, gpre_ref, wg_ref, wu_ref, wo_ref, gpost_ref, o_ref, hn_ref, acc_ref):
    k = pl.program_id(1)

    @pl.when(k == 0)
    def _():
        hn_ref[...] = (_rms(x_ref[...], EPS) * gpre_ref[...]).astype(BF16)
        acc_ref[...] = jnp.zeros(acc_ref.shape, F32)

    hn = hn_ref[...]
    gate = _dot(hn, wg_ref[...])
    up = _dot(hn, wu_ref[...])
    f = gate * _sigmoid(gate) * up
    acc_ref[...] += _dot(f.astype(BF16), wo_ref[...])

    @pl.when(k == pl.num_programs(1) - 1)
    def _():
        o_ref[...] = x_ref[...] + _rms(acc_ref[...], EPS) * gpost_ref[...]


def _ffn(x1, g_pre, w_in, w_out, g_post):
    m = x1.shape[0]
    tm = min(512, m)
    tf = 512
    nk = D_FF // tf
    row = pl.BlockSpec((tm, D_MODEL), lambda i, k: (i, 0))
    vec = pl.BlockSpec((1, D_MODEL), lambda i, k: (0, 0))
    return pl.pallas_call(
        _ffn_kernel,
        out_shape=jax.ShapeDtypeStruct((m, D_MODEL), F32),
        grid=(m // tm, nk),
        in_specs=[
            row, vec,
            pl.BlockSpec((D_MODEL, tf), lambda i, k: (0, k)),
            pl.BlockSpec((D_MODEL, tf), lambda i, k: (0, nk + k)),
            pl.BlockSpec((tf, D_MODEL), lambda i, k: (k, 0)),
            vec,
        ],
        out_specs=row,
        scratch_shapes=[pltpu.VMEM((tm, D_MODEL), BF16), pltpu.VMEM((tm, D_MODEL), F32)],
        compiler_params=_params(2),
        name="ffn",
    )(x1, g_pre, w_in, w_in, w_out, g_post)


def _pad_lora_cols(w, axis):
    o2, o3 = 3 * D_MODEL, 3 * D_MODEL + RW_DECAY_RANK
    o4 = o3 + RW_AAA_RANK
    take = lambda a, b: lax.slice_in_dim(w, a, b, axis=axis)
    pad_shape = list(w.shape)
    pad_shape[axis] = LORA_PAD - RW_DECAY_RANK
    pad = jnp.zeros(pad_shape, w.dtype)
    return jnp.concatenate([take(0, o2), take(o2, o3), pad, take(o3, o4), pad,
                            take(o4, o4 + RW_GATE_RANK)], axis=axis)


def _pad_rows(w, rows):
    return jnp.concatenate([w, jnp.zeros((rows - w.shape[0], w.shape[1]), w.dtype)], axis=0)


def _rope_tables(seq):
    half = DA_HEAD_DIM // 2
    inv = ROPE_THETA ** (-jnp.arange(half, dtype=F32) / half)
    ang = jnp.arange(seq, dtype=jnp.int32).astype(F32)[:, None] * inv[None, :]
    cos, sin = jnp.cos(ang), jnp.sin(ang)
    return jnp.concatenate([cos, cos], axis=1), jnp.concatenate([-sin, sin], axis=1)


def kernel(x, w_in, b_gate, g_mix_pre, g_mix_post, g_ffn_pre, g_ffn_post,
           da_lambda_q, da_lambda_k, da_subln_g,
           rw_mu, rw_w0, rw_w_up, rw_a0, rw_a_up, rw_g_up, rw_k_k, rw_k_a, rw_r_k,
           rw_ln_w, rw_ln_b, w_branch_a, w_branch_b, w_out, w_ffn_in, w_ffn_out):
    batch, seq, _ = x.shape
    assert w_in.shape[0] == 1
    assert seq % 512 == 0 and (seq <= 1024 or seq % 1024 == 0), "sequence must tile by the stage blocks"
    m = batch * seq
    x2 = x.reshape(m, D_MODEL)
    row = lambda t: t.reshape(1, -1).astype(F32)

    w_bf = w_in[0].astype(BF16)
    o3 = N_MAIN + RW_DECAY_RANK
    o4 = o3 + RW_AAA_RANK
    pad = jnp.zeros((D_MODEL, LORA_PAD - RW_DECAY_RANK), BF16)
    w_rest = jnp.concatenate([w_bf[:, N_MAIN:o3], pad, w_bf[:, o3:o4], pad, w_bf[:, o4:]], axis=1)
    assert w_rest.shape[1] == N_REST
    cos_t, sin_t = _rope_tables(seq)

    g_pre = row(g_mix_pre[0])
    pm = _proj_main(x2, g_pre, w_bf, cos_t, sin_t, seq)
    px = _proj_rest(x2, g_pre, w_rest)
    o_a = _attn(pm, da_lambda_q[0], da_lambda_k[0], row(da_subln_g[0]), batch, seq)

    rt, at, bt, kt, bh, kh, v, g, bonus, gl = _prep(
        pm, px, _pad_lora_cols(row(rw_mu[0]), 1), row(rw_w0[0]),
        _pad_rows(rw_w_up[0], LORA_PAD).astype(BF16), row(rw_a0[0]),
        _pad_rows(rw_a_up[0], LORA_PAD).astype(BF16), rw_g_up[0].astype(BF16),
        row(rw_k_k[0]), row(rw_k_a[0]), row(rw_r_k[0]), seq)
    y = _chunk(rt, at, bt, kt, bh, kh, v, gl, batch, seq)
    o_b = _post(y, bonus, g, row(rw_ln_w[0]), row(rw_ln_b[0]))

    x1 = _merge(o_a, o_b, px, row(b_gate[0]), w_branch_a[0].astype(BF16),
                w_branch_b[0].astype(BF16), w_out[0].astype(BF16), x2, row(g_mix_post[0]))
    out = _ffn(x1, row(g_ffn_pre[0]), w_ffn_in[0].astype(BF16), w_ffn_out[0].astype(BF16),
               row(g_ffn_post[0]))
    return out.reshape(batch, seq, D_MODEL)
```
